```python
import math
import jax, jax.numpy as jnp
from jax import lax
import numpy as np


D_MODEL = 1024
BATCH = 2
SEQ = 8192
DEPTH = 1
DEC_BATCH = 32
DEC_SEQ = 32
PAST_LEN = 2048

CHUNK = 64
Q_BLOCK = 128
MIX_WIDTH = D_MODEL
ATTN_WIDTH = MIX_WIDTH // 2
POOL_WIDTH = MIX_WIDTH - ATTN_WIDTH
N_HEADS = 4
HEAD_DIM = ATTN_WIDTH // (2 * N_HEADS)
ROT_DIM = HEAD_DIM // 4
ROPE_THETA = 500000.0
POOL_WINDOWS = (2, 4, 8, 16)
N_POOL_GROUPS = len(POOL_WINDOWS)
POOL_GC = POOL_WIDTH // N_POOL_GROUPS
POOL_HIST = max(POOL_WINDOWS) - 1
IN_WIDTH = 4 * ATTN_WIDTH + 2 * POOL_WIDTH
NORM_EPS = 1e-6
SUBLN_EPS = 1e-5
NEG_INF = -1e30

kernel_name = "hybrid_diffattn_pool_stream_step"


def rmsnorm(x, g, eps=NORM_EPS):
    xf = x.astype(jnp.float32)
    y = xf * lax.rsqrt(jnp.mean(xf * xf, axis=-1, keepdims=True) + eps) * g.astype(jnp.float32)
    return y.astype(x.dtype)


def rope(x, pos):
    inv = ROPE_THETA ** (-jnp.arange(0, ROT_DIM, 2, dtype=jnp.float32) / ROT_DIM)
    ang = pos.astype(jnp.float32)[:, None] * inv[None, :]
    cos = jnp.cos(ang)[None, :, None, None, :]
    sin = jnp.sin(ang)[None, :, None, None, :]
    xf = x.astype(jnp.float32)
    a = xf[..., :ROT_DIM // 2]
    b = xf[..., ROT_DIM // 2:ROT_DIM]
    out = jnp.concatenate([a * cos - b * sin, b * cos + a * sin, xf[..., ROT_DIM:]], axis=-1)
    return out.astype(x.dtype)


def _diff_attn_block(qb, qpos_b, k, v, kpos, lam):
    s = jnp.einsum('bqhjd,bkhjd->bhjqk', qb.astype(jnp.float32), k.astype(jnp.float32)) * (HEAD_DIM ** -0.5)
    mask = (kpos[None, :] // CHUNK) <= (qpos_b[:, None] // CHUNK)
    s = jnp.where(mask[None, None, None], s, NEG_INF)
    p = jax.nn.softmax(s, axis=-1)
    a = p[:, :, 0] - lam * p[:, :, 1]
    return jnp.einsum('bhqk,bkhe->bqhe', a, v.astype(jnp.float32))


def diff_attention(q, k, v, qpos, kpos, lam):
    B, S = q.shape[0], q.shape[1]
    if S > Q_BLOCK and S % Q_BLOCK == 0:
        nb = S // Q_BLOCK
        qb = q.reshape(B, nb, Q_BLOCK, N_HEADS, 2, HEAD_DIM).transpose(1, 0, 2, 3, 4, 5)
        pb = qpos.reshape(nb, Q_BLOCK)
        ob = lax.map(lambda args: _diff_attn_block(args[0], args[1], k, v, kpos, lam), (qb, pb))
        return ob.transpose(1, 0, 2, 3, 4).reshape(B, S, N_HEADS, 2 * HEAD_DIM)
    return _diff_attn_block(q, qpos, k, v, kpos, lam)


def pool_mix(u, hist, pos, w_pool, pool_scale):
    B, S, C = u.shape
    uf = u.astype(jnp.float32)
    z = jnp.concatenate([hist.astype(jnp.float32), uf], axis=1)
    c = jnp.concatenate([jnp.zeros((B, 1, C), jnp.float32), jnp.cumsum(z, axis=1)], axis=1)
    end = c[:, POOL_HIST + 1:POOL_HIST + 1 + S]
    means = []
    for g, w in enumerate(POOL_WINDOWS):
        lo, hi = g * POOL_GC, (g + 1) * POOL_GC
        start = c[:, POOL_HIST + 1 - w:POOL_HIST + 1 - w + S, lo:hi]
        cnt = jnp.minimum(pos + 1, w).astype(jnp.float32)[None, :, None]
        means.append((end[..., lo:hi] - start) / cnt)
    m = jnp.concatenate(means, axis=-1) - uf
    m = jnp.einsum('bsgc,gcd->bsgd', m.reshape(B, S, N_POOL_GROUPS, POOL_GC),
                   w_pool.astype(jnp.float32)).reshape(B, S, C)
    return (m * pool_scale.astype(jnp.float32)).astype(u.dtype)


def hybrid_layer(x, pos, k_hist, v_hist, kpos_all, pool_hist, norm_g, w_in, lq1, lk1, lq2, lk2,
                 subln_g, w_pool, pool_scale, w_out, lam_init):
    B, S, _ = x.shape
    h = rmsnorm(x, norm_g)
    proj = jnp.einsum('bsd,de->bse', h, w_in)
    q, k, v, ga, u, gp = jnp.split(proj, [ATTN_WIDTH, 2 * ATTN_WIDTH, 3 * ATTN_WIDTH,
                                          4 * ATTN_WIDTH, 4 * ATTN_WIDTH + POOL_WIDTH], axis=-1)
    q = rope(q.reshape(B, S, N_HEADS, 2, HEAD_DIM), pos)
    k = rope(k.reshape(B, S, N_HEADS, 2, HEAD_DIM), pos)
    k_rows = k.reshape(B, S, N_HEADS, 2 * HEAD_DIM)
    v_rows = v.reshape(B, S, N_HEADS, 2 * HEAD_DIM)
    k_all = jnp.concatenate([k_hist, k_rows], axis=1)
    v_all = jnp.concatenate([v_hist, v_rows], axis=1)
    Lk = k_all.shape[1]
    lam = (jnp.exp(jnp.sum(lq1.astype(jnp.float32) * lk1.astype(jnp.float32)))
           - jnp.exp(jnp.sum(lq2.astype(jnp.float32) * lk2.astype(jnp.float32))) + lam_init)
    o = diff_attention(q, k_all.reshape(B, Lk, N_HEADS, 2, HEAD_DIM), v_all, pos, kpos_all, lam)
    o = rmsnorm(o, subln_g, SUBLN_EPS) * (1.0 - lam_init)
    a_out = o.reshape(B, S, ATTN_WIDTH).astype(x.dtype) * jax.nn.silu(ga)
    p_out = pool_mix(u, pool_hist, pos, w_pool, pool_scale) * jax.nn.silu(gp)
    y = x + jnp.einsum('bse,ed->bsd', jnp.concatenate([a_out, p_out], axis=-1), w_out)
    new_pool = jnp.concatenate([pool_hist.astype(u.dtype), u], axis=1)[:, -POOL_HIST:]
    return y, k_rows, v_rows, new_pool


def setup_inputs(seed: int = 0) -> dict:
    key = jax.random.key(seed)
    ks = jax.random.split(key, 18)
    nrm = jax.random.normal
    f32 = jnp.float32
    return {
        "x_prompt": nrm(ks[0], (BATCH, SEQ, D_MODEL), f32),
        "x_sample": nrm(ks[1], (DEC_BATCH, DEC_SEQ, D_MODEL), f32),
        "cache_k": nrm(ks[2], (DEPTH, DEC_BATCH, PAST_LEN, N_HEADS, 2 * HEAD_DIM), f32),
        "cache_v": nrm(ks[3], (DEPTH, DEC_BATCH, PAST_LEN, N_HEADS, 2 * HEAD_DIM), f32),
        "state_pool": nrm(ks[4], (DEPTH, DEC_BATCH, POOL_HIST, POOL_WIDTH), f32),
        "norm_g": 1.0 + 0.05 * nrm(ks[5], (DEPTH, D_MODEL), f32),
        "w_in": nrm(ks[6], (DEPTH, D_MODEL, IN_WIDTH), f32) * D_MODEL ** -0.5,
        "lambda_q1": 0.1 * nrm(ks[7], (DEPTH, HEAD_DIM), f32),
        "lambda_k1": 0.1 * nrm(ks[8], (DEPTH, HEAD_DIM), f32),
        "lambda_q2": 0.1 * nrm(ks[9], (DEPTH, HEAD_DIM), f32),
        "lambda_k2": 0.1 * nrm(ks[10], (DEPTH, HEAD_DIM), f32),
        "subln_g": 1.0 + 0.05 * nrm(ks[11], (DEPTH, 2 * HEAD_DIM), f32),
        "w_pool": nrm(ks[12], (DEPTH, N_POOL_GROUPS, POOL_GC, POOL_GC), f32) * POOL_GC ** -0.5,
        "pool_scale": 1.0 + 0.05 * nrm(ks[13], (DEPTH, POOL_WIDTH), f32),
        "w_out": nrm(ks[14], (DEPTH, MIX_WIDTH, D_MODEL), f32) * MIX_WIDTH ** -0.5,
        "final_g": 1.0 + 0.05 * nrm(ks[15], (D_MODEL,), f32),
    }


def reference(x_prompt, x_sample, cache_k, cache_v, state_pool, norm_g, w_in, lambda_q1, lambda_k1,
              lambda_q2, lambda_k2, subln_g, w_pool, pool_scale, w_out, final_g):
    B, S, _ = x_prompt.shape
    DB, DS, _ = x_sample.shape
    L = cache_k.shape[2]
    pos_p = jnp.arange(S, dtype=jnp.int32)
    pos_s = L + jnp.arange(DS, dtype=jnp.int32)
    kpos_s = jnp.concatenate([jnp.arange(L, dtype=jnp.int32), pos_s])
    xp, xs = x_prompt, x_sample
    kp_l, vp_l, pp_l, ks_l, vs_l, ps_l = [], [], [], [], [], []
    for l in range(DEPTH):
        lam_init = 0.8 - 0.6 * math.exp(-0.3 * l)
        shared = (norm_g[l], w_in[l], lambda_q1[l], lambda_k1[l], lambda_q2[l], lambda_k2[l],
                  subln_g[l], w_pool[l], pool_scale[l], w_out[l], lam_init)
        empty_kv = jnp.zeros((B, 0, N_HEADS, 2 * HEAD_DIM), xp.dtype)
        zero_pool = jnp.zeros((B, POOL_HIST, POOL_WIDTH), xp.dtype)
        xp, kp, vp, pp = hybrid_layer(xp, pos_p, empty_kv, empty_kv, pos_p, zero_pool, *shared)
        xs, kn, vn, pn = hybrid_layer(xs, pos_s, cache_k[l], cache_v[l], kpos_s, state_pool[l], *shared)
        kp_l.append(kp); vp_l.append(vp); pp_l.append(pp)
        ks_l.append(kn); vs_l.append(vn); ps_l.append(pn)
    y_prompt = rmsnorm(xp, final_g)
    y_sample = rmsnorm(xs, final_g)
    k_prompt = jnp.stack(kp_l, axis=0)
    v_prompt = jnp.stack(vp_l, axis=0)
    pool_prompt = jnp.stack(pp_l, axis=0)
    k_sample = jnp.stack(ks_l, axis=0)
    v_sample = jnp.stack(vs_l, axis=0)
    pool_sample = jnp.stack(ps_l, axis=0)
    return (y_prompt, y_sample, k_prompt, v_prompt, pool_prompt, k_sample, v_sample, pool_sample)
```

```python
import functools
import math

import jax
import jax.numpy as jnp
from jax import lax
from jax.experimental import pallas as pl
from jax.experimental.pallas import tpu as pltpu

CHUNK = 64
N_HEADS = 4
HEAD_DIM = 64
HEAD_W = 2 * HEAD_DIM
ROT_DIM = HEAD_DIM // 4
ROT_HALF = ROT_DIM // 2
ROPE_THETA = 500000.0
POOL_WINDOWS = (2, 4, 8, 16)
POOL_HIST = max(POOL_WINDOWS) - 1
HIST_ROWS = POOL_HIST + 1
NORM_EPS = 1e-6
SUBLN_EPS = 1e-5
NEG_INF = -1e30

LANES = 128
VMEM_LIMIT_BYTES = 56 * 1024 * 1024

PROJ_ROWS = 512
ATTN_TQ = 256
ATTN_TK = 256
SAMPLE_SEQS = 8

_NT = (((1,), (1,)), ((), ()))


def _params(*semantics):
    return pltpu.CompilerParams(dimension_semantics=semantics, vmem_limit_bytes=VMEM_LIMIT_BYTES)


def _silu(x):
    return x / (1.0 + jnp.exp(-x))


def _lam(lq1_ref, lk1_ref, lq2_ref, lk2_ref, lam_init):
    a = jnp.sum(lq1_ref[...] * lk1_ref[...], axis=-1, keepdims=True)
    b = jnp.sum(lq2_ref[...] * lk2_ref[...], axis=-1, keepdims=True)
    return jnp.exp(a) - jnp.exp(b) + lam_init


def _rope_head(xh, cos, sin_lo, sin_hi):
    up = pltpu.roll(xh, LANES - ROT_HALF, axis=1)
    down = pltpu.roll(xh, ROT_HALF, axis=1)
    return xh * cos + up * sin_lo + down * sin_hi


def _proj_kernel(x_ref, g_ref, w_ref, cos_ref, slo_ref, shi_ref, *out_refs, transposed, tk):
    width = N_HEADS * HEAD_W
    x = x_ref[...]
    ms = jnp.mean(x * x, axis=-1, keepdims=True)
    h = (x * lax.rsqrt(ms + NORM_EPS) * g_ref[...]).astype(jnp.bfloat16)

    def seg(i):
        return jnp.dot(h, w_ref[:, i * width:(i + 1) * width], preferred_element_type=jnp.float32)

    cos, slo, shi = cos_ref[...], slo_ref[...], shi_ref[...]
    q, k, v = seg(0), seg(1), seg(2)
    if transposed:
        k_ref, v_ref, ga_ref, u_ref, gp_ref, qt_ref, kb_ref, vt_ref = out_refs
    else:
        k_ref, v_ref, ga_ref, u_ref, gp_ref, q_ref = out_refs
    v_ref[...] = v
    ga_ref[...] = seg(3)
    u_ref[...] = seg(4)
    gp_ref[...] = seg(5)
    rows = x.shape[0]
    for hd in range(N_HEADS):
        lanes = slice(hd * HEAD_W, (hd + 1) * HEAD_W)
        kh = _rope_head(k[:, lanes], cos, slo, shi)
        qh = _rope_head(q[:, lanes], cos, slo, shi) * (HEAD_DIM ** -0.5)
        k_ref[:, lanes] = kh
        if transposed:
            kb_ref[hd] = kh.astype(jnp.bfloat16)
            qt_ref[hd] = qh.T.astype(jnp.bfloat16)
            vt = v[:, lanes].T.astype(jnp.bfloat16)
            for c in range(rows // tk):
                vt_ref[hd, c] = vt[:, c * tk:(c + 1) * tk]
        else:
            q_ref[:, lanes] = qh.astype(jnp.bfloat16)


def _rope_tables(pos):
    inv = ROPE_THETA ** (-jnp.arange(0, ROT_DIM, 2, dtype=jnp.float32) / ROT_DIM)
    ang = pos.astype(jnp.float32)[:, None] * inv[None, :]
    d = jnp.arange(LANES) % HEAD_DIM
    idx = d % ROT_HALF
    cos = jnp.where(d[None, :] < ROT_DIM, jnp.cos(ang)[:, idx], 1.0)
    sin = jnp.sin(ang)[:, idx]
    sin_lo = jnp.where(d[None, :] < ROT_HALF, -sin, 0.0)
    sin_hi = jnp.where((d[None, :] >= ROT_HALF) & (d[None, :] < ROT_DIM), sin, 0.0)
    return cos, sin_lo, sin_hi


def _project(x2d, norm_g, w_in_bf16, tables, n_table_blocks, *, tm, transposed, tk):
    rows, d_model = x2d.shape
    width = N_HEADS * HEAD_W
    in_width = w_in_bf16.shape[1]
    f32, bf16 = jnp.float32, jnp.bfloat16
    row_spec = pl.BlockSpec((tm, width), lambda i: (i, 0))
    tab_spec = pl.BlockSpec((tm, LANES), lambda i: (i % n_table_blocks, 0))
    out_shape = [jax.ShapeDtypeStruct((rows, width), f32)] * 5
    out_specs = [row_spec] * 5
    if transposed:
        out_shape += [
            jax.ShapeDtypeStruct((N_HEADS, HEAD_W, rows), bf16),
            jax.ShapeDtypeStruct((N_HEADS, rows, HEAD_W), bf16),
            jax.ShapeDtypeStruct((N_HEADS, rows // tk, HEAD_W, tk), bf16),
        ]
        out_specs += [
            pl.BlockSpec((N_HEADS, HEAD_W, tm), lambda i: (0, 0, i)),
            pl.BlockSpec((N_HEADS, tm, HEAD_W), lambda i: (0, i, 0)),
            pl.BlockSpec((N_HEADS, tm // tk, HEAD_W, tk), lambda i: (0, i, 0, 0)),
        ]
    else:
        out_shape += [jax.ShapeDtypeStruct((rows, width), bf16)]
        out_specs += [row_spec]
    return pl.pallas_call(
        functools.partial(_proj_kernel, transposed=transposed, tk=tk),
        grid=(rows // tm,),
        in_specs=[
            pl.BlockSpec((tm, d_model), lambda i: (i, 0)),
            pl.BlockSpec((1, d_model), lambda i: (0, 0)),
            pl.BlockSpec((d_model, in_width), lambda i: (0, 0)),
            tab_spec, tab_spec, tab_spec,
        ],
        out_specs=out_specs,
        out_shape=out_shape,
        compiler_params=_params("parallel"),
        name="proj_t" if transposed else "proj_n",
    )(x2d, norm_g.reshape(1, d_model), w_in_bf16, *tables)


def _attn_prompt_kernel(lq1_ref, lk1_ref, lq2_ref, lk2_ref, g_ref, qt_ref, k_ref, vt_ref, o_ref,
                        acc_ref, *, tq, tk, lam_init):
    qi = pl.program_id(2)
    qt = qt_ref[0]
    row = lax.broadcasted_iota(jnp.int32, qt.shape, 0)
    zero = jnp.zeros_like(qt)
    q2t = jnp.concatenate([jnp.where(row < HEAD_DIM, qt, zero),
                           jnp.where(row >= HEAD_DIM, qt, zero)], axis=1)
    acc_ref[...] = jnp.zeros_like(acc_ref)

    def step(tile, m, l, mask):
        k = k_ref[0, pl.ds(pl.multiple_of(tile * tk, tk), tk), :]
        st = jnp.dot(k, q2t, preferred_element_type=jnp.float32)
        if mask is not None:
            st = jnp.where(mask, st, NEG_INF)
        m_new = jnp.maximum(m, jnp.max(st, axis=0, keepdims=True))
        alpha = jnp.exp(m - m_new)
        pt = jnp.exp(st - m_new)
        l_new = alpha * l + jnp.sum(pt, axis=0, keepdims=True)
        pv = jnp.dot(vt_ref[0, tile], pt.astype(jnp.bfloat16), preferred_element_type=jnp.float32)
        acc_ref[...] = alpha * acc_ref[...] + pv
        return m_new, l_new

    sub = tq // tk
    m0 = jnp.full((1, 2 * tq), NEG_INF, jnp.float32)
    l0 = jnp.zeros((1, 2 * tq), jnp.float32)
    m, l = lax.fori_loop(0, qi * sub, lambda t, c: step(t, c[0], c[1], None), (m0, l0))
    kchunk = lax.broadcasted_iota(jnp.int32, (tk, 2 * tq), 0) // CHUNK
    col = lax.broadcasted_iota(jnp.int32, (tk, 2 * tq), 1)
    qchunk = jnp.where(col >= tq, col - tq, col) // CHUNK
    for j in range(sub):
        m, l = step(qi * sub + j, m, l, (kchunk + j * (tk // CHUNK)) <= qchunk)

    lam = _lam(lq1_ref, lk1_ref, lq2_ref, lk2_ref, lam_init)
    acc = acc_ref[...] / l
    ot = acc[:, :tq] - lam * acc[:, tq:]
    ms = jnp.mean(ot * ot, axis=0, keepdims=True)
    y = ot * lax.rsqrt(ms + SUBLN_EPS) * g_ref[...] * (1.0 - lam_init)
    o_ref[...] = y.T


def _attend_prompt(lams, subln_g, qt, kb, vt, *, batch, seq, lam_init):
    tq, tk = ATTN_TQ, ATTN_TK
    rows = batch * seq
    nq = seq // tq
    lam_spec = pl.BlockSpec((1, HEAD_DIM), lambda b, h, i: (0, 0))
    return pl.pallas_call(
        functools.partial(_attn_prompt_kernel, tq=tq, tk=tk, lam_init=lam_init),
        grid=(batch, N_HEADS, nq),
        in_specs=[
            lam_spec, lam_spec, lam_spec, lam_spec,
            pl.BlockSpec((HEAD_W, 1), lambda b, h, i: (0, 0)),
            pl.BlockSpec((1, HEAD_W, tq), lambda b, h, i: (h, 0, b * nq + i)),
            pl.BlockSpec((1, seq, HEAD_W), lambda b, h, i: (h, b, 0)),
            pl.BlockSpec((1, seq // tk, HEAD_W, tk), lambda b, h, i: (h, b, 0, 0)),
        ],
        out_specs=pl.BlockSpec((tq, HEAD_W), lambda b, h, i: (b * nq + i, h)),
        out_shape=jax.ShapeDtypeStruct((rows, N_HEADS * HEAD_W), jnp.float32),
        scratch_shapes=[pltpu.VMEM((HEAD_W, 2 * tq), jnp.float32)],
        compiler_params=_params("parallel", "parallel", "arbitrary"),
        name="attn_prompt",
    )(*lams, subln_g.reshape(HEAD_W, 1), qt, kb, vt)


def _attn_sample_kernel(lq1_ref, lk1_ref, lq2_ref, lk2_ref, g_ref, q_ref, ck_ref, cv_ref, kn_ref,
                        vn_ref, o_ref, *, lam_init):
    bf16 = jnp.bfloat16
    lam = _lam(lq1_ref, lk1_ref, lq2_ref, lk2_ref, lam_init)
    ds = q_ref.shape[0]
    for hd in range(N_HEADS):
        lanes = slice(hd * HEAD_W, (hd + 1) * HEAD_W)
        qh = q_ref[:, lanes]
        lane = lax.broadcasted_iota(jnp.int32, qh.shape, 1)
        zero = jnp.zeros_like(qh)
        q2 = jnp.concatenate([jnp.where(lane < HEAD_DIM, qh, zero),
                              jnp.where(lane >= HEAD_DIM, qh, zero)], axis=0)
        s_c = lax.dot_general(q2, ck_ref[:, lanes].astype(bf16), _NT,
                              preferred_element_type=jnp.float32)
        s_n = lax.dot_general(q2, kn_ref[:, lanes].astype(bf16), _NT,
                              preferred_element_type=jnp.float32)
        m = jnp.maximum(jnp.max(s_c, axis=-1, keepdims=True), jnp.max(s_n, axis=-1, keepdims=True))
        p_c = jnp.exp(s_c - m)
        p_n = jnp.exp(s_n - m)
        l = jnp.sum(p_c, axis=-1, keepdims=True) + jnp.sum(p_n, axis=-1, keepdims=True)
        acc = (jnp.dot(p_c.astype(bf16), cv_ref[:, lanes].astype(bf16), preferred_element_type=jnp.float32)
               + jnp.dot(p_n.astype(bf16), vn_ref[:, lanes].astype(bf16), preferred_element_type=jnp.float32))
        acc = acc / l
        o = acc[:ds] - lam * acc[ds:]
        ms = jnp.mean(o * o, axis=-1, keepdims=True)
        o_ref[:, lanes] = o * lax.rsqrt(ms + SUBLN_EPS) * g_ref[...] * (1.0 - lam_init)


def _attend_sample(lams, subln_g, q, cache_k, cache_v, k_new, v_new, *, layer, lam_init):
    _, db, past, _, _ = cache_k.shape
    width = N_HEADS * HEAD_W
    ds = q.shape[0] // db
    assert (past + ds - 1) // CHUNK <= past // CHUNK
    ck = cache_k.reshape(cache_k.shape[0], db, past, width)
    cv = cache_v.reshape(cache_v.shape[0], db, past, width)
    lam_spec = pl.BlockSpec((1, HEAD_DIM), lambda b: (0, 0))
    row_spec = pl.BlockSpec((ds, width), lambda b: (b, 0))
    cache_spec = pl.BlockSpec((None, None, past, width), lambda b: (layer, b, 0, 0))
    return pl.pallas_call(
        functools.partial(_attn_sample_kernel, lam_init=lam_init),
        grid=(db,),
        in_specs=[lam_spec, lam_spec, lam_spec, lam_spec,
                  pl.BlockSpec((1, HEAD_W), lambda b: (0, 0)),
                  row_spec, cache_spec, cache_spec, row_spec, row_spec],
        out_specs=row_spec,
        out_shape=jax.ShapeDtypeStruct((db * ds, width), jnp.float32),
        compiler_params=_params("parallel"),
        name="attn_sample",
    )(*lams, subln_g.reshape(1, HEAD_W), q, ck, cv, k_new, v_new)


def _mix_out_kernel(x_ref, o_ref, ga_ref, u_ref, hist_ref, gp_ref, wpool_ref, pscale_ref, wout_ref,
                    fg_ref, y_ref, z_ref, *, zero_first_hist, pos_base, pos_stride, final):
    nb, rows, _ = u_ref.shape
    hist = hist_ref[...]
    if zero_first_hist:
        hist = jnp.where(pl.program_id(1) == 0, 0.0, hist)
    u = u_ref[...]
    z_ref[:, :HIST_ROWS, :] = hist
    z_ref[:, HIST_ROWS:, :] = u
    pos = (pos_base + pl.program_id(1) * pos_stride
           + lax.broadcasted_iota(jnp.int32, (nb, rows, LANES), 1))
    parts = []
    for g, w in enumerate(POOL_WINDOWS):
        lanes = slice(g * LANES, (g + 1) * LANES)
        ug = u[:, :, lanes]
        wsum = ug
        for back in range(1, w):
            wsum = wsum + z_ref[:, HIST_ROWS - back:HIST_ROWS - back + rows, lanes]
        cnt = jnp.minimum(pos + 1, w).astype(jnp.float32)
        mg = (wsum / cnt - ug).reshape(nb * rows, LANES).astype(jnp.bfloat16)
        parts.append(jnp.dot(mg, wpool_ref[g], preferred_element_type=jnp.float32))
    p_out = jnp.concatenate(parts, axis=-1) * pscale_ref[...] * _silu(gp_ref[...])
    a_out = o_ref[...] * _silu(ga_ref[...])
    mixed = jnp.concatenate([a_out, p_out], axis=-1).astype(jnp.bfloat16)
    y = x_ref[...] + jnp.dot(mixed, wout_ref[...], preferred_element_type=jnp.float32)
    if final:
        ms = jnp.mean(y * y, axis=-1, keepdims=True)
        y = y * lax.rsqrt(ms + NORM_EPS) * fg_ref[...]
    y_ref[...] = y


def _mix_out(x2d, o, ga, u, hist, gp, w_pool_bf16, pool_scale, w_out_bf16, final_g, *, n_seq, seq,
             nb, rows, hist_from_u, pos_base, final):
    d_model = x2d.shape[1]
    width = N_HEADS * HEAD_W
    g0, g1 = n_seq // nb, seq // rows
    tm = nb * rows
    u3 = u.reshape(n_seq, seq, width)
    row_map = lambda a, b: (a * g1 + b, 0)
    const2 = lambda a, b: (0, 0)
    if hist_from_u:
        hist, hist_blocks = u3, rows // HIST_ROWS
        hist_spec = pl.BlockSpec((nb, HIST_ROWS, width),
                                 lambda a, b: (a, jnp.maximum(b * hist_blocks - 1, 0), 0))
    else:
        hist_spec = pl.BlockSpec((nb, HIST_ROWS, width), lambda a, b: (a, 0, 0))
    return pl.pallas_call(
        functools.partial(_mix_out_kernel, zero_first_hist=hist_from_u, pos_base=pos_base,
                          pos_stride=rows if hist_from_u else 0, final=final),
        grid=(g0, g1),
        in_specs=[
            pl.BlockSpec((tm, d_model), row_map),
            pl.BlockSpec((tm, width), row_map),
            pl.BlockSpec((tm, width), row_map),
            pl.BlockSpec((nb, rows, width), lambda a, b: (a, b, 0)),
            hist_spec,
            pl.BlockSpec((tm, width), row_map),
            pl.BlockSpec(w_pool_bf16.shape, lambda a, b: (0, 0, 0)),
            pl.BlockSpec((1, width), const2),
            pl.BlockSpec(w_out_bf16.shape, const2),
            pl.BlockSpec((1, d_model), const2),
        ],
        out_specs=pl.BlockSpec((tm, d_model), row_map),
        out_shape=jax.ShapeDtypeStruct(x2d.shape, jnp.float32),
        scratch_shapes=[pltpu.VMEM((nb, HIST_ROWS + rows, width), jnp.float32)],
        compiler_params=_params("parallel", "arbitrary"),
        name="mix_out",
    )(x2d, o, ga, u3, hist, gp, w_pool_bf16, pool_scale.reshape(1, width), w_out_bf16,
      final_g.reshape(1, d_model))


def kernel(x_prompt, x_sample, cache_k, cache_v, state_pool, norm_g, w_in, lambda_q1, lambda_k1,
           lambda_q2, lambda_k2, subln_g, w_pool, pool_scale, w_out, final_g):
    batch, seq, d_model = x_prompt.shape
    db, ds, _ = x_sample.shape
    depth = w_in.shape[0]
    past = cache_k.shape[2]
    width = N_HEADS * HEAD_W
    bf16 = jnp.bfloat16

    tm_s = SAMPLE_SEQS * ds
    tab_p = _rope_tables(jnp.arange(seq, dtype=jnp.int32))
    tab_s = _rope_tables(past + (jnp.arange(tm_s, dtype=jnp.int32) % ds))

    xp = x_prompt.reshape(batch * seq, d_model)
    xs = x_sample.reshape(db * ds, d_model)
    outs = {name: [] for name in ("kp", "vp", "pp", "ks", "vs", "ps")}
    for layer in range(depth):
        lam_init = 0.8 - 0.6 * math.exp(-0.3 * layer)
        final = layer == depth - 1
        w_in_b = w_in[layer].astype(bf16)
        w_pool_b = w_pool[layer].astype(bf16)
        w_out_b = w_out[layer].astype(bf16)
        lams = [p[layer].reshape(1, HEAD_DIM) for p in (lambda_q1, lambda_k1, lambda_q2, lambda_k2)]

        k, v, ga, u, gp, qt, kb, vt = _project(
            xp, norm_g[layer], w_in_b, tab_p, seq // PROJ_ROWS, tm=PROJ_ROWS, transposed=True, tk=ATTN_TK)
        o = _attend_prompt(lams, subln_g[layer], qt, kb, vt, batch=batch, seq=seq, lam_init=lam_init)
        xp = _mix_out(xp, o, ga, u, None, gp, w_pool_b, pool_scale[layer], w_out_b, final_g,
                      n_seq=batch, seq=seq, nb=1, rows=PROJ_ROWS, hist_from_u=True, pos_base=0, final=final)
        outs["kp"].append(k.reshape(batch, seq, N_HEADS, HEAD_W))
        outs["vp"].append(v.reshape(batch, seq, N_HEADS, HEAD_W))
        outs["pp"].append(u.reshape(batch, seq, width)[:, seq - POOL_HIST:])

        k, v, ga, u, gp, q = _project(
            xs, norm_g[layer], w_in_b, tab_s, 1, tm=tm_s, transposed=False, tk=ATTN_TK)
        o = _attend_sample(lams, subln_g[layer], q, cache_k, cache_v, k, v, layer=layer, lam_init=lam_init)
        hist = jnp.pad(state_pool[layer], ((0, 0), (HIST_ROWS - POOL_HIST, 0), (0, 0)))
        xs = _mix_out(xs, o, ga, u, hist, gp, w_pool_b, pool_scale[layer], w_out_b, final_g,
                      n_seq=db, seq=ds, nb=SAMPLE_SEQS, rows=ds, hist_from_u=False, pos_base=past,
                      final=final)
        outs["ks"].append(k.reshape(db, ds, N_HEADS, HEAD_W))
        outs["vs"].append(v.reshape(db, ds, N_HEADS, HEAD_W))
        outs["ps"].append(u.reshape(db, ds, width)[:, ds - POOL_HIST:])

    stack = lambda name: jnp.stack(outs[name], axis=0)
    return (xp.reshape(batch, seq, d_model), xs.reshape(db, ds, d_model), stack("kp"), stack("vp"),
            stack("pp"), stack("ks"), stack("vs"), stack("ps"))
```

```python
import functools
import math

import jax
import jax.numpy as jnp
from jax import lax
from jax.experimental import pallas as pl
from jax.experimental.pallas import tpu as pltpu

CHUNK = 64
N_HEADS = 4
HEAD_DIM = 64
HEAD_W = 2 * HEAD_DIM
ROT_DIM = HEAD_DIM // 4
ROT_HALF = ROT_DIM // 2
ROPE_THETA = 500000.0
POOL_WINDOWS = (2, 4, 8, 16)
POOL_HIST = max(POOL_WINDOWS) - 1
HIST_ROWS = POOL_HIST + 1
NORM_EPS = 1e-6
SUBLN_EPS = 1e-5
NEG_INF = -1e30

LANES = 128
VMEM_LIMIT_BYTES = 56 * 1024 * 1024

PROJ_ROWS = 512
ATTN_TQ = 256
ATTN_TK = 256
SAMPLE_SEQS = 8

_NT = (((1,), (1,)), ((), ()))


def _params(*semantics):
    return pltpu.CompilerParams(dimension_semantics=semantics, vmem_limit_bytes=VMEM_LIMIT_BYTES)


def _silu(x):
    return x / (1.0 + jnp.exp(-x))


def _lam(lq1_ref, lk1_ref, lq2_ref, lk2_ref, lam_init):
    a = jnp.sum(lq1_ref[...] * lk1_ref[...], axis=-1, keepdims=True)
    b = jnp.sum(lq2_ref[...] * lk2_ref[...], axis=-1, keepdims=True)
    return jnp.exp(a) - jnp.exp(b) + lam_init


def _rope_head(xh, cos, sin_lo, sin_hi):
    up = pltpu.roll(xh, LANES - ROT_HALF, axis=1)
    down = pltpu.roll(xh, ROT_HALF, axis=1)
    return xh * cos + up * sin_lo + down * sin_hi


def _proj_kernel(x_ref, g_ref, w_ref, cos_ref, slo_ref, shi_ref, *out_refs, transposed, tk):
    width = N_HEADS * HEAD_W
    x = x_ref[...]
    ms = jnp.mean(x * x, axis=-1, keepdims=True)
    h = (x * lax.rsqrt(ms + NORM_EPS) * g_ref[...]).astype(jnp.bfloat16)

    def seg(i):
        return jnp.dot(h, w_ref[:, i * width:(i + 1) * width], preferred_element_type=jnp.float32)

    cos, slo, shi = cos_ref[...], slo_ref[...], shi_ref[...]
    q, k, v = seg(0), seg(1), seg(2)
    if transposed:
        k_ref, v_ref, ga_ref, u_ref, gp_ref, qt_ref, kb_ref, vt_ref = out_refs
    else:
        k_ref, v_ref, ga_ref, u_ref, gp_ref, q_ref = out_refs
    ga_ref[...] = seg(3)
    u_ref[...] = seg(4)
    gp_ref[...] = seg(5)
    rows = x.shape[0]
    for hd in range(N_HEADS):
        lanes = slice(hd * HEAD_W, (hd + 1) * HEAD_W)
        kh = _rope_head(k[:, lanes], cos, slo, shi)
        qh = _rope_head(q[:, lanes], cos, slo, shi) * (
            HEAD_DIM ** -0.5 * (math.log2(math.e) if transposed else 1.0))
        k_ref[pl.ds(hd, rows, stride=N_HEADS), :] = kh
        v_ref[pl.ds(hd, rows, stride=N_HEADS), :] = v[:, lanes]
        if transposed:
            kb_ref[hd] = kh.astype(jnp.bfloat16)
            qt_ref[hd] = qh.T.astype(jnp.bfloat16)
            vt = v[:, lanes].T.astype(jnp.bfloat16)
            for c in range(rows // tk):
                vt_ref[hd, c] = vt[:, c * tk:(c + 1) * tk]
        else:
            q_ref[:, lanes] = qh.astype(jnp.bfloat16)


def _rope_tables(pos):
    inv = ROPE_THETA ** (-jnp.arange(0, ROT_DIM, 2, dtype=jnp.float32) / ROT_DIM)
    ang = pos.astype(jnp.float32)[:, None] * inv[None, :]
    d = jnp.arange(LANES) % HEAD_DIM
    idx = d % ROT_HALF
    cos = jnp.where(d[None, :] < ROT_DIM, jnp.cos(ang)[:, idx], 1.0)
    sin = jnp.sin(ang)[:, idx]
    sin_lo = jnp.where(d[None, :] < ROT_HALF, -sin, 0.0)
    sin_hi = jnp.where((d[None, :] >= ROT_HALF) & (d[None, :] < ROT_DIM), sin, 0.0)
    return cos, sin_lo, sin_hi


def _project(x2d, norm_g, w_in_bf16, tables, n_table_blocks, *, tm, transposed, tk):
    rows, d_model = x2d.shape
    width = N_HEADS * HEAD_W
    in_width = w_in_bf16.shape[1]
    f32, bf16 = jnp.float32, jnp.bfloat16
    row_spec = pl.BlockSpec((tm, width), lambda i: (i, 0))
    tab_spec = pl.BlockSpec((tm, LANES), lambda i: (i % n_table_blocks, 0))
    kv_spec = pl.BlockSpec((tm * N_HEADS, HEAD_W), lambda i: (i, 0))
    out_shape = ([jax.ShapeDtypeStruct((rows * N_HEADS, HEAD_W), f32)] * 2
                 + [jax.ShapeDtypeStruct((rows, width), f32)] * 3)
    out_specs = [kv_spec] * 2 + [row_spec] * 3
    if transposed:
        out_shape += [
            jax.ShapeDtypeStruct((N_HEADS, HEAD_W, rows), bf16),
            jax.ShapeDtypeStruct((N_HEADS, rows, HEAD_W), bf16),
            jax.ShapeDtypeStruct((N_HEADS, rows // tk, HEAD_W, tk), bf16),
        ]
        out_specs += [
            pl.BlockSpec((N_HEADS, HEAD_W, tm), lambda i: (0, 0, i)),
            pl.BlockSpec((N_HEADS, tm, HEAD_W), lambda i: (0, i, 0)),
            pl.BlockSpec((N_HEADS, tm // tk, HEAD_W, tk), lambda i: (0, i, 0, 0)),
        ]
    else:
        out_shape += [jax.ShapeDtypeStruct((rows, width), bf16)]
        out_specs += [row_spec]
    return pl.pallas_call(
        functools.partial(_proj_kernel, transposed=transposed, tk=tk),
        grid=(rows // tm,),
        in_specs=[
            pl.BlockSpec((tm, d_model), lambda i: (i, 0)),
            pl.BlockSpec((1, d_model), lambda i: (0, 0)),
            pl.BlockSpec((d_model, in_width), lambda i: (0, 0)),
            tab_spec, tab_spec, tab_spec,
        ],
        out_specs=out_specs,
        out_shape=out_shape,
        compiler_params=_params("parallel"),
        name="proj_t" if transposed else "proj_n",
    )(x2d, norm_g.reshape(1, d_model), w_in_bf16, *tables)


def _attn_prompt_kernel(lq1_ref, lk1_ref, lq2_ref, lk2_ref, g_ref, qt_ref, k_ref, vt_ref, o_ref,
                        acc_ref, *, tq, tk, lam_init):
    qi = pl.program_id(2)
    qt = qt_ref[0]
    row = lax.broadcasted_iota(jnp.int32, qt.shape, 0)
    zero = jnp.zeros_like(qt)
    q2t = jnp.concatenate([jnp.where(row < HEAD_DIM, qt, zero),
                           jnp.where(row >= HEAD_DIM, qt, zero)], axis=1)
    acc_ref[...] = jnp.zeros_like(acc_ref)

    def step(tile, m, l, mask):
        k = k_ref[0, pl.ds(pl.multiple_of(tile * tk, tk), tk), :]
        st = jnp.dot(k, q2t, preferred_element_type=jnp.float32)
        if mask is not None:
            st = jnp.where(mask, st, NEG_INF)
        m_new = jnp.maximum(m, jnp.max(st, axis=0, keepdims=True))
        alpha = jnp.exp2(m - m_new)
        pt = jnp.exp2(st - m_new)
        l_new = alpha * l + jnp.sum(pt, axis=0, keepdims=True)
        pv = jnp.dot(vt_ref[0, tile], pt.astype(jnp.bfloat16), preferred_element_type=jnp.float32)
        acc_ref[...] = alpha * acc_ref[...] + pv
        return m_new, l_new

    sub = tq // tk
    m0 = jnp.full((1, 2 * tq), NEG_INF, jnp.float32)
    l0 = jnp.zeros((1, 2 * tq), jnp.float32)
    m, l = lax.fori_loop(0, qi * sub, lambda t, c: step(t, c[0], c[1], None), (m0, l0))
    kchunk = lax.broadcasted_iota(jnp.int32, (tk, 2 * tq), 0) // CHUNK
    col = lax.broadcasted_iota(jnp.int32, (tk, 2 * tq), 1)
    qchunk = jnp.where(col >= tq, col - tq, col) // CHUNK
    for j in range(sub):
        m, l = step(qi * sub + j, m, l, (kchunk + j * (tk // CHUNK)) <= qchunk)

    lam = _lam(lq1_ref, lk1_ref, lq2_ref, lk2_ref, lam_init)
    acc = acc_ref[...] / l
    ot = acc[:, :tq] - lam * acc[:, tq:]
    ms = jnp.mean(ot * ot, axis=0, keepdims=True)
    y = ot * lax.rsqrt(ms + SUBLN_EPS) * g_ref[...] * (1.0 - lam_init)
    o_ref[...] = y.T


def _attend_prompt(lams, subln_g, qt, kb, vt, *, batch, seq, lam_init):
    tq, tk = ATTN_TQ, ATTN_TK
    rows = batch * seq
    nq = seq // tq
    lam_spec = pl.BlockSpec((1, HEAD_DIM), lambda b, h, i: (0, 0))
    return pl.pallas_call(
        functools.partial(_attn_prompt_kernel, tq=tq, tk=tk, lam_init=lam_init),
        grid=(batch, N_HEADS, nq),
        in_specs=[
            lam_spec, lam_spec, lam_spec, lam_spec,
            pl.BlockSpec((HEAD_W, 1), lambda b, h, i: (0, 0)),
            pl.BlockSpec((1, HEAD_W, tq), lambda b, h, i: (h, 0, b * nq + i)),
            pl.BlockSpec((1, seq, HEAD_W), lambda b, h, i: (h, b, 0)),
            pl.BlockSpec((1, seq // tk, HEAD_W, tk), lambda b, h, i: (h, b, 0, 0)),
        ],
        out_specs=pl.BlockSpec((tq, HEAD_W), lambda b, h, i: (b * nq + i, h)),
        out_shape=jax.ShapeDtypeStruct((rows, N_HEADS * HEAD_W), jnp.float32),
        scratch_shapes=[pltpu.VMEM((HEAD_W, 2 * tq), jnp.float32)],
        compiler_params=_params("parallel", "parallel", "arbitrary"),
        name="attn_prompt",
    )(*lams, subln_g.reshape(HEAD_W, 1), qt, kb, vt)


def _attn_sample_kernel(lq1_ref, lk1_ref, lq2_ref, lk2_ref, g_ref, q_ref, ck_ref, cv_ref, kn_ref,
                        vn_ref, o_ref, *, lam_init):
    bf16 = jnp.bfloat16
    lam = _lam(lq1_ref, lk1_ref, lq2_ref, lk2_ref, lam_init)
    ds = q_ref.shape[0]
    past = ck_ref.shape[0] // N_HEADS

    def head_rows(ref, hd, n):
        return ref[pl.ds(hd, n, stride=N_HEADS), :].astype(bf16)

    for hd in range(N_HEADS):
        lanes = slice(hd * HEAD_W, (hd + 1) * HEAD_W)
        qh = q_ref[:, lanes]
        lane = lax.broadcasted_iota(jnp.int32, qh.shape, 1)
        zero = jnp.zeros_like(qh)
        q2 = jnp.concatenate([jnp.where(lane < HEAD_DIM, qh, zero),
                              jnp.where(lane >= HEAD_DIM, qh, zero)], axis=0)
        s_c = lax.dot_general(q2, head_rows(ck_ref, hd, past), _NT,
                              preferred_element_type=jnp.float32)
        s_n = lax.dot_general(q2, head_rows(kn_ref, hd, ds), _NT,
                              preferred_element_type=jnp.float32)
        m = jnp.maximum(jnp.max(s_c, axis=-1, keepdims=True), jnp.max(s_n, axis=-1, keepdims=True))
        p_c = jnp.exp(s_c - m)
        p_n = jnp.exp(s_n - m)
        l = jnp.sum(p_c, axis=-1, keepdims=True) + jnp.sum(p_n, axis=-1, keepdims=True)
        acc = (jnp.dot(p_c.astype(bf16), head_rows(cv_ref, hd, past), preferred_element_type=jnp.float32)
               + jnp.dot(p_n.astype(bf16), head_rows(vn_ref, hd, ds), preferred_element_type=jnp.float32))
        acc = acc / l
        o = acc[:ds] - lam * acc[ds:]
        ms = jnp.mean(o * o, axis=-1, keepdims=True)
        o_ref[:, lanes] = o * lax.rsqrt(ms + SUBLN_EPS) * g_ref[...] * (1.0 - lam_init)


def _attend_sample(lams, subln_g, q, cache_k, cache_v, k_new, v_new, *, layer, lam_init):
    _, db, past, _, _ = cache_k.shape
    width = N_HEADS * HEAD_W
    ds = q.shape[0] // db
    assert (past + ds - 1) // CHUNK <= past // CHUNK
    lam_spec = pl.BlockSpec((1, HEAD_DIM), lambda b: (0, 0))
    row_spec = pl.BlockSpec((ds, width), lambda b: (b, 0))
    new_spec = pl.BlockSpec((ds * N_HEADS, HEAD_W), lambda b: (b, 0))
    cache_spec = pl.BlockSpec((None, None, past * N_HEADS, HEAD_W), lambda b: (layer, b, 0, 0))
    slab = lambda c: c.reshape(c.shape[0], db, past * N_HEADS, HEAD_W)
    return pl.pallas_call(
        functools.partial(_attn_sample_kernel, lam_init=lam_init),
        grid=(db,),
        in_specs=[lam_spec, lam_spec, lam_spec, lam_spec,
                  pl.BlockSpec((1, HEAD_W), lambda b: (0, 0)),
                  row_spec, cache_spec, cache_spec, new_spec, new_spec],
        out_specs=row_spec,
        out_shape=jax.ShapeDtypeStruct((db * ds, width), jnp.float32),
        compiler_params=_params("parallel"),
        name="attn_sample",
    )(*lams, subln_g.reshape(1, HEAD_W), q, slab(cache_k), slab(cache_v), k_new, v_new)


def _mix_out_kernel(x_ref, o_ref, ga_ref, u_ref, hist_ref, gp_ref, wpool_ref, pscale_ref, wout_ref,
                    fg_ref, y_ref, z_ref, *, zero_first_hist, pos_base, pos_stride, final):
    nb, rows, _ = u_ref.shape
    hist = hist_ref[...]
    if zero_first_hist:
        hist = jnp.where(pl.program_id(1) == 0, 0.0, hist)
    u = u_ref[...]
    z_ref[:, :HIST_ROWS, :] = hist
    z_ref[:, HIST_ROWS:, :] = u
    pos = (pos_base + pl.program_id(1) * pos_stride
           + lax.broadcasted_iota(jnp.int32, (nb, rows, LANES), 1))
    parts = []
    for g, w in enumerate(POOL_WINDOWS):
        lanes = slice(g * LANES, (g + 1) * LANES)
        ug = u[:, :, lanes]
        wsum = ug
        for back in range(1, w):
            wsum = wsum + z_ref[:, HIST_ROWS - back:HIST_ROWS - back + rows, lanes]
        cnt = jnp.minimum(pos + 1, w).astype(jnp.float32)
        mg = (wsum / cnt - ug).reshape(nb * rows, LANES).astype(jnp.bfloat16)
        parts.append(jnp.dot(mg, wpool_ref[g], preferred_element_type=jnp.float32))
    p_out = jnp.concatenate(parts, axis=-1) * pscale_ref[...] * _silu(gp_ref[...])
    a_out = o_ref[...] * _silu(ga_ref[...])
    mixed = jnp.concatenate([a_out, p_out], axis=-1).astype(jnp.bfloat16)
    y = x_ref[...] + jnp.dot(mixed, wout_ref[...], preferred_element_type=jnp.float32)
    if final:
        ms = jnp.mean(y * y, axis=-1, keepdims=True)
        y = y * lax.rsqrt(ms + NORM_EPS) * fg_ref[...]
    y_ref[...] = y


def _mix_out(x2d, o, ga, u, hist, gp, w_pool_bf16, pool_scale, w_out_bf16, final_g, *, n_seq, seq,
             nb, rows, hist_from_u, pos_base, final):
    d_model = x2d.shape[1]
    width = N_HEADS * HEAD_W
    g0, g1 = n_seq // nb, seq // rows
    tm = nb * rows
    u3 = u.reshape(n_seq, seq, width)
    row_map = lambda a, b: (a * g1 + b, 0)
    const2 = lambda a, b: (0, 0)
    if hist_from_u:
        hist, hist_blocks = u3, rows // HIST_ROWS
        hist_spec = pl.BlockSpec((nb, HIST_ROWS, width),
                                 lambda a, b: (a, jnp.maximum(b * hist_blocks - 1, 0), 0))
    else:
        hist_spec = pl.BlockSpec((nb, HIST_ROWS, width), lambda a, b: (a, 0, 0))
    return pl.pallas_call(
        functools.partial(_mix_out_kernel, zero_first_hist=hist_from_u, pos_base=pos_base,
                          pos_stride=rows if hist_from_u else 0, final=final),
        grid=(g0, g1),
        in_specs=[
            pl.BlockSpec((tm, d_model), row_map),
            pl.BlockSpec((tm, width), row_map),
            pl.BlockSpec((tm, width), row_map),
            pl.BlockSpec((nb, rows, width), lambda a, b: (a, b, 0)),
            hist_spec,
            pl.BlockSpec((tm, width), row_map),
            pl.BlockSpec(w_pool_bf16.shape, lambda a, b: (0, 0, 0)),
            pl.BlockSpec((1, width), const2),
            pl.BlockSpec(w_out_bf16.shape, const2),
            pl.BlockSpec((1, d_model), const2),
        ],
        out_specs=pl.BlockSpec((tm, d_model), row_map),
        out_shape=jax.ShapeDtypeStruct(x2d.shape, jnp.float32),
        scratch_shapes=[pltpu.VMEM((nb, HIST_ROWS + rows, width), jnp.float32)],
        compiler_params=_params("parallel", "arbitrary"),
        name="mix_out",
    )(x2d, o, ga, u3, hist, gp, w_pool_bf16, pool_scale.reshape(1, width), w_out_bf16,
      final_g.reshape(1, d_model))


def kernel(x_prompt, x_sample, cache_k, cache_v, state_pool, norm_g, w_in, lambda_q1, lambda_k1,
           lambda_q2, lambda_k2, subln_g, w_pool, pool_scale, w_out, final_g):
    batch, seq, d_model = x_prompt.shape
    db, ds, _ = x_sample.shape
    depth = w_in.shape[0]
    past = cache_k.shape[2]
    width = N_HEADS * HEAD_W
    bf16 = jnp.bfloat16

    tm_s = SAMPLE_SEQS * ds
    tab_p = _rope_tables(jnp.arange(seq, dtype=jnp.int32))
    tab_s = _rope_tables(past + (jnp.arange(tm_s, dtype=jnp.int32) % ds))

    xp = x_prompt.reshape(batch * seq, d_model)
    xs = x_sample.reshape(db * ds, d_model)
    outs = {name: [] for name in ("kp", "vp", "pp", "ks", "vs", "ps")}
    for layer in range(depth):
        lam_init = 0.8 - 0.6 * math.exp(-0.3 * layer)
        final = layer == depth - 1
        w_in_b = w_in[layer].astype(bf16)
        w_pool_b = w_pool[layer].astype(bf16)
        w_out_b = w_out[layer].astype(bf16)
        lams = [p[layer].reshape(1, HEAD_DIM) for p in (lambda_q1, lambda_k1, lambda_q2, lambda_k2)]

        k, v, ga, u, gp, qt, kb, vt = _project(
            xp, norm_g[layer], w_in_b, tab_p, seq // PROJ_ROWS, tm=PROJ_ROWS, transposed=True, tk=ATTN_TK)
        o = _attend_prompt(lams, subln_g[layer], qt, kb, vt, batch=batch, seq=seq, lam_init=lam_init)
        xp = _mix_out(xp, o, ga, u, None, gp, w_pool_b, pool_scale[layer], w_out_b, final_g,
                      n_seq=batch, seq=seq, nb=1, rows=PROJ_ROWS, hist_from_u=True, pos_base=0, final=final)
        outs["kp"].append(k.reshape(batch, seq, N_HEADS, HEAD_W))
        outs["vp"].append(v.reshape(batch, seq, N_HEADS, HEAD_W))
        outs["pp"].append(u.reshape(batch, seq, width)[:, seq - POOL_HIST:])

        k, v, ga, u, gp, q = _project(
            xs, norm_g[layer], w_in_b, tab_s, 1, tm=tm_s, transposed=False, tk=ATTN_TK)
        o = _attend_sample(lams, subln_g[layer], q, cache_k, cache_v, k, v, layer=layer, lam_init=lam_init)
        hist = jnp.pad(state_pool[layer], ((0, 0), (HIST_ROWS - POOL_HIST, 0), (0, 0)))
        xs = _mix_out(xs, o, ga, u, hist, gp, w_pool_b, pool_scale[layer], w_out_b, final_g,
                      n_seq=db, seq=ds, nb=SAMPLE_SEQS, rows=ds, hist_from_u=False, pos_base=past,
                      final=final)
        outs["ks"].append(k.reshape(db, ds, N_HEADS, HEAD_W))
        outs["vs"].append(v.reshape(db, ds, N_HEADS, HEAD_W))
        outs["ps"].append(u.reshape(db, ds, width)[:, ds - POOL_HIST:])

    stack = lambda name: jnp.stack(outs[name], axis=0)
    return (xp.reshape(batch, seq, d_model), xs.reshape(db, ds, d_model), stack("kp"), stack("vp"),
            stack("pp"), stack("ks"), stack("vs"), stack("ps"))
```

```python
import functools
import math

import jax
import jax.numpy as jnp
from jax import lax
from jax.experimental import pallas as pl
from jax.experimental.pallas import tpu as pltpu

CHUNK = 64
N_HEADS = 4
HEAD_DIM = 64
HEAD_W = 2 * HEAD_DIM
ROT_DIM = HEAD_DIM // 4
ROT_HALF = ROT_DIM // 2
ROPE_THETA = 500000.0
POOL_WINDOWS = (2, 4, 8, 16)
POOL_HIST = max(POOL_WINDOWS) - 1
HIST_ROWS = POOL_HIST + 1
NORM_EPS = 1e-6
SUBLN_EPS = 1e-5
NEG_INF = -1e30

LANES = 128
VMEM_LIMIT_BYTES = 56 * 1024 * 1024

PROJ_ROWS = 512
ATTN_TILE = PROJ_ROWS
ATTN_HEADS = 2
M_FLOOR = -1e29
SAMPLE_SEQS = 8

_NT = (((1,), (1,)), ((), ()))


def _params(*semantics):
    return pltpu.CompilerParams(dimension_semantics=semantics, vmem_limit_bytes=VMEM_LIMIT_BYTES)


def _silu(x):
    return x / (1.0 + jnp.exp(-x))


def _lam(lq1_ref, lk1_ref, lq2_ref, lk2_ref, lam_init):
    a = jnp.sum(lq1_ref[...] * lk1_ref[...], axis=-1, keepdims=True)
    b = jnp.sum(lq2_ref[...] * lk2_ref[...], axis=-1, keepdims=True)
    return jnp.exp(a) - jnp.exp(b) + lam_init


def _rope_head(xh, cos, sin_lo, sin_hi):
    up = pltpu.roll(xh, LANES - ROT_HALF, axis=1)
    down = pltpu.roll(xh, ROT_HALF, axis=1)
    return xh * cos + up * sin_lo + down * sin_hi


def _proj_kernel(x_ref, g_ref, w_ref, cos_ref, slo_ref, shi_ref, *out_refs, transposed):
    width = N_HEADS * HEAD_W
    x = x_ref[...]
    ms = jnp.mean(x * x, axis=-1, keepdims=True)
    h = (x * lax.rsqrt(ms + NORM_EPS) * g_ref[...]).astype(jnp.bfloat16)

    def seg(i):
        return jnp.dot(h, w_ref[:, i * width:(i + 1) * width], preferred_element_type=jnp.float32)

    cos, slo, shi = cos_ref[...], slo_ref[...], shi_ref[...]
    q, k, v = seg(0), seg(1), seg(2)
    if transposed:
        k_ref, v_ref, ga_ref, u_ref, gp_ref, qt_ref, kb_ref, vt_ref = out_refs
    else:
        k_ref, v_ref, ga_ref, u_ref, gp_ref, q_ref = out_refs
    ga_ref[...] = seg(3)
    u_ref[...] = seg(4)
    gp_ref[...] = seg(5)
    rows = x.shape[0]
    for hd in range(N_HEADS):
        lanes = slice(hd * HEAD_W, (hd + 1) * HEAD_W)
        kh = _rope_head(k[:, lanes], cos, slo, shi)
        qh = _rope_head(q[:, lanes], cos, slo, shi) * (
            HEAD_DIM ** -0.5 * (math.log2(math.e) if transposed else 1.0))
        k_ref[pl.ds(hd, rows, stride=N_HEADS), :] = kh
        v_ref[pl.ds(hd, rows, stride=N_HEADS), :] = v[:, lanes]
        if transposed:
            kb_ref[hd] = kh.astype(jnp.bfloat16)
            vt_ref[hd, 0] = v[:, lanes].T.astype(jnp.bfloat16)
            qt = qh.T.astype(jnp.bfloat16)
            first_half = lax.broadcasted_iota(jnp.int32, qt.shape, 0) < HEAD_DIM
            zero = jnp.zeros_like(qt)
            qt_ref[hd, 0, :, :rows] = jnp.where(first_half, qt, zero)
            qt_ref[hd, 0, :, rows:] = jnp.where(first_half, zero, qt)
        else:
            q_ref[:, lanes] = qh.astype(jnp.bfloat16)


def _rope_tables(pos):
    inv = ROPE_THETA ** (-jnp.arange(0, ROT_DIM, 2, dtype=jnp.float32) / ROT_DIM)
    ang = pos.astype(jnp.float32)[:, None] * inv[None, :]
    d = jnp.arange(LANES) % HEAD_DIM
    idx = d % ROT_HALF
    cos = jnp.where(d[None, :] < ROT_DIM, jnp.cos(ang)[:, idx], 1.0)
    sin = jnp.sin(ang)[:, idx]
    sin_lo = jnp.where(d[None, :] < ROT_HALF, -sin, 0.0)
    sin_hi = jnp.where((d[None, :] >= ROT_HALF) & (d[None, :] < ROT_DIM), sin, 0.0)
    return cos, sin_lo, sin_hi


def _project(x2d, norm_g, w_in_bf16, tables, n_table_blocks, *, tm, transposed):
    rows, d_model = x2d.shape
    width = N_HEADS * HEAD_W
    in_width = w_in_bf16.shape[1]
    f32, bf16 = jnp.float32, jnp.bfloat16
    row_spec = pl.BlockSpec((tm, width), lambda i: (i, 0))
    tab_spec = pl.BlockSpec((tm, LANES), lambda i: (i % n_table_blocks, 0))
    kv_spec = pl.BlockSpec((tm * N_HEADS, HEAD_W), lambda i: (i, 0))
    out_shape = ([jax.ShapeDtypeStruct((rows * N_HEADS, HEAD_W), f32)] * 2
                 + [jax.ShapeDtypeStruct((rows, width), f32)] * 3)
    out_specs = [kv_spec] * 2 + [row_spec] * 3
    if transposed:
        out_shape += [
            jax.ShapeDtypeStruct((N_HEADS, rows // tm, HEAD_W, 2 * tm), bf16),
            jax.ShapeDtypeStruct((N_HEADS, rows, HEAD_W), bf16),
            jax.ShapeDtypeStruct((N_HEADS, rows // tm, HEAD_W, tm), bf16),
        ]
        out_specs += [
            pl.BlockSpec((N_HEADS, 1, HEAD_W, 2 * tm), lambda i: (0, i, 0, 0)),
            pl.BlockSpec((N_HEADS, tm, HEAD_W), lambda i: (0, i, 0)),
            pl.BlockSpec((N_HEADS, 1, HEAD_W, tm), lambda i: (0, i, 0, 0)),
        ]
    else:
        out_shape += [jax.ShapeDtypeStruct((rows, width), bf16)]
        out_specs += [row_spec]
    return pl.pallas_call(
        functools.partial(_proj_kernel, transposed=transposed),
        grid=(rows // tm,),
        in_specs=[
            pl.BlockSpec((tm, d_model), lambda i: (i, 0)),
            pl.BlockSpec((1, d_model), lambda i: (0, 0)),
            pl.BlockSpec((d_model, in_width), lambda i: (0, 0)),
            tab_spec, tab_spec, tab_spec,
        ],
        out_specs=out_specs,
        out_shape=out_shape,
        compiler_params=_params("parallel"),
        name="proj_t" if transposed else "proj_n",
    )(x2d, norm_g.reshape(1, d_model), w_in_bf16, *tables)


def _attn_prompt_kernel(lq1_ref, lk1_ref, lq2_ref, lk2_ref, g_ref, qt_ref, k_ref, vt_ref, o_ref,
                        s_ref, mc_ref, m_ref, l_ref, acc_ref, bias_ref, *, t, lam_init):
    nh, nq = qt_ref.shape[0], qt_ref.shape[1]
    n_items = nq * (nq + 1) // 2
    bf16 = jnp.bfloat16

    kchunk = lax.broadcasted_iota(jnp.int32, (t, 2 * t), 0) // CHUNK
    col = lax.broadcasted_iota(jnp.int32, (t, 2 * t), 1)
    qchunk = jnp.where(col >= t, col - t, col) // CHUNK
    bias_ref[0] = jnp.zeros((t, 2 * t), jnp.float32)
    bias_ref[1] = jnp.where(kchunk <= qchunk, 0.0, NEG_INF)
    acc_ref[...] = jnp.zeros_like(acc_ref)
    l_ref[...] = jnp.zeros_like(l_ref)

    def scores(qa, ka, slot):
        diag = (ka == qa).astype(jnp.int32)
        koff = pl.multiple_of(ka * t, t)
        for hd in range(nh):
            st = jnp.dot(k_ref[hd, pl.ds(koff, t), :], qt_ref[hd, qa],
                         preferred_element_type=jnp.float32) + bias_ref[diag]
            s_ref[slot, hd] = st
            mc_ref[slot, hd] = jnp.max(st, axis=0, keepdims=True)

    def finish_tile(qb):
        lam = _lam(lq1_ref, lk1_ref, lq2_ref, lk2_ref, lam_init)
        rows = pl.ds(pl.multiple_of(qb * t, t), t)
        for hd in range(nh):
            acc = acc_ref[hd] / l_ref[hd]
            ot = acc[:, :t] - lam * acc[:, t:]
            ms = jnp.mean(ot * ot, axis=0, keepdims=True)
            y = ot * lax.rsqrt(ms + SUBLN_EPS) * g_ref[...] * (1.0 - lam_init)
            o_ref[rows, hd * HEAD_W:(hd + 1) * HEAD_W] = y.T

    def accumulate(qb, kb, slot):
        first = kb == 0
        for hd in range(nh):
            m_prev = jnp.where(first, M_FLOOR, m_ref[hd])
            m_new = jnp.maximum(m_prev, mc_ref[slot, hd])
            alpha = jnp.exp2(m_prev - m_new)
            pt = jnp.exp2(s_ref[slot, hd] - m_new)
            l_ref[hd] = alpha * l_ref[hd] + jnp.sum(pt, axis=0, keepdims=True)
            m_ref[hd] = m_new
            pv = jnp.dot(vt_ref[hd, kb], pt.astype(bf16), preferred_element_type=jnp.float32)
            acc_ref[hd] = alpha * acc_ref[hd] + pv
        pl.when(kb == qb)(lambda: finish_tile(qb))

    def following(q, k):
        wrap = k >= q
        qn = jnp.where(wrap, q + 1, q)
        kn = jnp.where(wrap, 0, k + 1)
        done = qn >= nq
        return jnp.where(done, nq - 1, qn), jnp.where(done, nq - 1, kn)

    def iteration(item, prev, slot):
        scores(*item, slot)
        accumulate(*prev, 1 - slot)
        return following(*item), item

    def pair(_, carry):
        item, prev = (carry[0], carry[1]), (carry[2], carry[3])
        item, prev = iteration(item, prev, 1)
        item, prev = iteration(item, prev, 0)
        return (*item, *prev)

    zero = jnp.int32(0)
    scores(zero, zero, 0)
    carry = lax.fori_loop(0, n_items // 2, pair, (*following(zero, zero), zero, zero))
    if n_items % 2:
        iteration((carry[0], carry[1]), (carry[2], carry[3]), 1)


def _attend_prompt(lams, subln_g, qt, kb, vt, *, batch, seq, lam_init):
    t, nh = ATTN_TILE, ATTN_HEADS
    rows = batch * seq
    nq = seq // t
    lam_spec = pl.BlockSpec((1, HEAD_DIM), lambda b, h: (0, 0))
    once = pl.Buffered(1)
    return pl.pallas_call(
        functools.partial(_attn_prompt_kernel, t=t, lam_init=lam_init),
        grid=(batch, N_HEADS // nh),
        in_specs=[
            lam_spec, lam_spec, lam_spec, lam_spec,
            pl.BlockSpec((HEAD_W, 1), lambda b, h: (0, 0)),
            pl.BlockSpec((nh, nq, HEAD_W, 2 * t), lambda b, h: (h, b, 0, 0), pipeline_mode=once),
            pl.BlockSpec((nh, seq, HEAD_W), lambda b, h: (h, b, 0), pipeline_mode=once),
            pl.BlockSpec((nh, nq, HEAD_W, t), lambda b, h: (h, b, 0, 0), pipeline_mode=once),
        ],
        out_specs=pl.BlockSpec((seq, nh * HEAD_W), lambda b, h: (b, h)),
        out_shape=jax.ShapeDtypeStruct((rows, N_HEADS * HEAD_W), jnp.float32),
        scratch_shapes=[
            pltpu.VMEM((2, nh, t, 2 * t), jnp.float32),
            pltpu.VMEM((2, nh, 1, 2 * t), jnp.float32),
            pltpu.VMEM((nh, 1, 2 * t), jnp.float32),
            pltpu.VMEM((nh, 1, 2 * t), jnp.float32),
            pltpu.VMEM((nh, HEAD_W, 2 * t), jnp.float32),
            pltpu.VMEM((2, t, 2 * t), jnp.float32),
        ],
        compiler_params=_params("parallel", "parallel"),
        name="attn_prompt",
    )(*lams, subln_g.reshape(HEAD_W, 1), qt, kb, vt)


def _attn_sample_kernel(lq1_ref, lk1_ref, lq2_ref, lk2_ref, g_ref, q_ref, ck_ref, cv_ref, kn_ref,
                        vn_ref, o_ref, *, lam_init):
    bf16 = jnp.bfloat16
    lam = _lam(lq1_ref, lk1_ref, lq2_ref, lk2_ref, lam_init)
    ds = q_ref.shape[0]
    past = ck_ref.shape[0] // N_HEADS

    def head_rows(ref, hd, n):
        return ref[pl.ds(hd, n, stride=N_HEADS), :].astype(bf16)

    for hd in range(N_HEADS):
        lanes = slice(hd * HEAD_W, (hd + 1) * HEAD_W)
        qh = q_ref[:, lanes]
        lane = lax.broadcasted_iota(jnp.int32, qh.shape, 1)
        zero = jnp.zeros_like(qh)
        q2 = jnp.concatenate([jnp.where(lane < HEAD_DIM, qh, zero),
                              jnp.where(lane >= HEAD_DIM, qh, zero)], axis=0)
        s_c = lax.dot_general(q2, head_rows(ck_ref, hd, past), _NT,
                              preferred_element_type=jnp.float32)
        s_n = lax.dot_general(q2, head_rows(kn_ref, hd, ds), _NT,
                              preferred_element_type=jnp.float32)
        m = jnp.maximum(jnp.max(s_c, axis=-1, keepdims=True), jnp.max(s_n, axis=-1, keepdims=True))
        p_c = jnp.exp(s_c - m)
        p_n = jnp.exp(s_n - m)
        l = jnp.sum(p_c, axis=-1, keepdims=True) + jnp.sum(p_n, axis=-1, keepdims=True)
        acc = (jnp.dot(p_c.astype(bf16), head_rows(cv_ref, hd, past), preferred_element_type=jnp.float32)
               + jnp.dot(p_n.astype(bf16), head_rows(vn_ref, hd, ds), preferred_element_type=jnp.float32))
        acc = acc / l
        o = acc[:ds] - lam * acc[ds:]
        ms = jnp.mean(o * o, axis=-1, keepdims=True)
        o_ref[:, lanes] = o * lax.rsqrt(ms + SUBLN_EPS) * g_ref[...] * (1.0 - lam_init)


def _attend_sample(lams, subln_g, q, cache_k, cache_v, k_new, v_new, *, layer, lam_init):
    _, db, past, _, _ = cache_k.shape
    width = N_HEADS * HEAD_W
    ds = q.shape[0] // db
    assert (past + ds - 1) // CHUNK <= past // CHUNK
    lam_spec = pl.BlockSpec((1, HEAD_DIM), lambda b: (0, 0))
    row_spec = pl.BlockSpec((ds, width), lambda b: (b, 0))
    new_spec = pl.BlockSpec((ds * N_HEADS, HEAD_W), lambda b: (b, 0))
    cache_spec = pl.BlockSpec((None, None, past * N_HEADS, HEAD_W), lambda b: (layer, b, 0, 0))
    slab = lambda c: c.reshape(c.shape[0], db, past * N_HEADS, HEAD_W)
    return pl.pallas_call(
        functools.partial(_attn_sample_kernel, lam_init=lam_init),
        grid=(db,),
        in_specs=[lam_spec, lam_spec, lam_spec, lam_spec,
                  pl.BlockSpec((1, HEAD_W), lambda b: (0, 0)),
                  row_spec, cache_spec, cache_spec, new_spec, new_spec],
        out_specs=row_spec,
        out_shape=jax.ShapeDtypeStruct((db * ds, width), jnp.float32),
        compiler_params=_params("parallel"),
        name="attn_sample",
    )(*lams, subln_g.reshape(1, HEAD_W), q, slab(cache_k), slab(cache_v), k_new, v_new)


def _mix_out_kernel(x_ref, o_ref, ga_ref, u_ref, hist_ref, gp_ref, wpool_ref, pscale_ref, wout_ref,
                    fg_ref, y_ref, z_ref, *, zero_first_hist, pos_base, pos_stride, final):
    nb, rows, _ = u_ref.shape
    hist = hist_ref[...]
    if zero_first_hist:
        hist = jnp.where(pl.program_id(1) == 0, 0.0, hist)
    u = u_ref[...]
    z_ref[:, :HIST_ROWS, :] = hist
    z_ref[:, HIST_ROWS:, :] = u
    pos = (pos_base + pl.program_id(1) * pos_stride
           + lax.broadcasted_iota(jnp.int32, (nb, rows, LANES), 1))
    parts = []
    for g, w in enumerate(POOL_WINDOWS):
        lanes = slice(g * LANES, (g + 1) * LANES)
        ug = u[:, :, lanes]
        wsum = ug
        for back in range(1, w):
            wsum = wsum + z_ref[:, HIST_ROWS - back:HIST_ROWS - back + rows, lanes]
        cnt = jnp.minimum(pos + 1, w).astype(jnp.float32)
        mg = (wsum / cnt - ug).reshape(nb * rows, LANES).astype(jnp.bfloat16)
        parts.append(jnp.dot(mg, wpool_ref[g], preferred_element_type=jnp.float32))
    p_out = jnp.concatenate(parts, axis=-1) * pscale_ref[...] * _silu(gp_ref[...])
    a_out = o_ref[...] * _silu(ga_ref[...])
    mixed = jnp.concatenate([a_out, p_out], axis=-1).astype(jnp.bfloat16)
    y = x_ref[...] + jnp.dot(mixed, wout_ref[...], preferred_element_type=jnp.float32)
    if final:
        ms = jnp.mean(y * y, axis=-1, keepdims=True)
        y = y * lax.rsqrt(ms + NORM_EPS) * fg_ref[...]
    y_ref[...] = y


def _mix_out(x2d, o, ga, u, hist, gp, w_pool_bf16, pool_scale, w_out_bf16, final_g, *, n_seq, seq,
             nb, rows, hist_from_u, pos_base, final):
    d_model = x2d.shape[1]
    width = N_HEADS * HEAD_W
    g0, g1 = n_seq // nb, seq // rows
    tm = nb * rows
    u3 = u.reshape(n_seq, seq, width)
    row_map = lambda a, b: (a * g1 + b, 0)
    const2 = lambda a, b: (0, 0)
    if hist_from_u:
        hist, hist_blocks = u3, rows // HIST_ROWS
        hist_spec = pl.BlockSpec((nb, HIST_ROWS, width),
                                 lambda a, b: (a, jnp.maximum(b * hist_blocks - 1, 0), 0))
    else:
        hist_spec = pl.BlockSpec((nb, HIST_ROWS, width), lambda a, b: (a, 0, 0))
    return pl.pallas_call(
        functools.partial(_mix_out_kernel, zero_first_hist=hist_from_u, pos_base=pos_base,
                          pos_stride=rows if hist_from_u else 0, final=final),
        grid=(g0, g1),
        in_specs=[
            pl.BlockSpec((tm, d_model), row_map),
            pl.BlockSpec((tm, width), row_map),
            pl.BlockSpec((tm, width), row_map),
            pl.BlockSpec((nb, rows, width), lambda a, b: (a, b, 0)),
            hist_spec,
            pl.BlockSpec((tm, width), row_map),
            pl.BlockSpec(w_pool_bf16.shape, lambda a, b: (0, 0, 0)),
            pl.BlockSpec((1, width), const2),
            pl.BlockSpec(w_out_bf16.shape, const2),
            pl.BlockSpec((1, d_model), const2),
        ],
        out_specs=pl.BlockSpec((tm, d_model), row_map),
        out_shape=jax.ShapeDtypeStruct(x2d.shape, jnp.float32),
        scratch_shapes=[pltpu.VMEM((nb, HIST_ROWS + rows, width), jnp.float32)],
        compiler_params=_params("parallel", "arbitrary"),
        name="mix_out",
    )(x2d, o, ga, u3, hist, gp, w_pool_bf16, pool_scale.reshape(1, width), w_out_bf16,
      final_g.reshape(1, d_model))


def kernel(x_prompt, x_sample, cache_k, cache_v, state_pool, norm_g, w_in, lambda_q1, lambda_k1,
           lambda_q2, lambda_k2, subln_g, w_pool, pool_scale, w_out, final_g):
    batch, seq, d_model = x_prompt.shape
    db, ds, _ = x_sample.shape
    depth = w_in.shape[0]
    past = cache_k.shape[2]
    width = N_HEADS * HEAD_W
    bf16 = jnp.bfloat16

    tm_s = SAMPLE_SEQS * ds
    tab_p = _rope_tables(jnp.arange(seq, dtype=jnp.int32))
    tab_s = _rope_tables(past + (jnp.arange(tm_s, dtype=jnp.int32) % ds))

    xp = x_prompt.reshape(batch * seq, d_model)
    xs = x_sample.reshape(db * ds, d_model)
    outs = {name: [] for name in ("kp", "vp", "pp", "ks", "vs", "ps")}
    for layer in range(depth):
        lam_init = 0.8 - 0.6 * math.exp(-0.3 * layer)
        final = layer == depth - 1
        w_in_b = w_in[layer].astype(bf16)
        w_pool_b = w_pool[layer].astype(bf16)
        w_out_b = w_out[layer].astype(bf16)
        lams = [p[layer].reshape(1, HEAD_DIM) for p in (lambda_q1, lambda_k1, lambda_q2, lambda_k2)]

        k, v, ga, u, gp, qt, kb, vt = _project(
            xp, norm_g[layer], w_in_b, tab_p, seq // ATTN_TILE, tm=ATTN_TILE, transposed=True)
        o = _attend_prompt(lams, subln_g[layer], qt, kb, vt, batch=batch, seq=seq, lam_init=lam_init)
        xp = _mix_out(xp, o, ga, u, None, gp, w_pool_b, pool_scale[layer], w_out_b, final_g,
                      n_seq=batch, seq=seq, nb=1, rows=PROJ_ROWS, hist_from_u=True, pos_base=0, final=final)
        outs["kp"].append(k.reshape(batch, seq, N_HEADS, HEAD_W))
        outs["vp"].append(v.reshape(batch, seq, N_HEADS, HEAD_W))
        outs["pp"].append(u.reshape(batch, seq, width)[:, seq - POOL_HIST:])

        k, v, ga, u, gp, q = _project(
            xs, norm_g[layer], w_in_b, tab_s, 1, tm=tm_s, transposed=False)
        o = _attend_sample(lams, subln_g[layer], q, cache_k, cache_v, k, v, layer=layer, lam_init=lam_init)
        hist = jnp.pad(state_pool[layer], ((0, 0), (HIST_ROWS - POOL_HIST, 0), (0, 0)))
        xs = _mix_out(xs, o, ga, u, hist, gp, w_pool_b, pool_scale[layer], w_out_b, final_g,
                      n_seq=db, seq=ds, nb=SAMPLE_SEQS, rows=ds, hist_from_u=False, pos_base=past,
                      final=final)
        outs["ks"].append(k.reshape(db, ds, N_HEADS, HEAD_W))
        outs["vs"].append(v.reshape(db, ds, N_HEADS, HEAD_W))
        outs["ps"].append(u.reshape(db, ds, width)[:, ds - POOL_HIST:])

    stack = lambda name: jnp.stack(outs[name], axis=0)
    return (xp.reshape(batch, seq, d_model), xs.reshape(db, ds, d_model), stack("kp"), stack("vp"),
            stack("pp"), stack("ks"), stack("vs"), stack("ps"))
```

```python
import functools
import math

import jax
import jax.numpy as jnp
from jax import lax
from jax.experimental import pallas as pl
from jax.experimental.pallas import tpu as pltpu

CHUNK = 64
N_HEADS = 4
HEAD_DIM = 64
HEAD_W = 2 * HEAD_DIM
ROT_DIM = HEAD_DIM // 4
ROT_HALF = ROT_DIM // 2
ROPE_THETA = 500000.0
POOL_WINDOWS = (2, 4, 8, 16)
POOL_HIST = max(POOL_WINDOWS) - 1
HIST_ROWS = POOL_HIST + 1
NORM_EPS = 1e-6
SUBLN_EPS = 1e-5
NEG_INF = -1e30

LANES = 128
MXU_COLS = 256
VMEM_LIMIT_BYTES = 56 * 1024 * 1024

PROJ_ROWS = 512
ATTN_TILE = PROJ_ROWS
ATTN_HEADS = 2
M_FLOOR = -1e29
SAMPLE_SEQS = 8

_NT = (((1,), (1,)), ((), ()))


def _params(*semantics):
    return pltpu.CompilerParams(dimension_semantics=semantics, vmem_limit_bytes=VMEM_LIMIT_BYTES)


def _silu(x):
    return x / (1.0 + jnp.exp(-x))


def _lam(lq1_ref, lk1_ref, lq2_ref, lk2_ref, lam_init):
    a = jnp.sum(lq1_ref[...] * lk1_ref[...], axis=-1, keepdims=True)
    b = jnp.sum(lq2_ref[...] * lk2_ref[...], axis=-1, keepdims=True)
    return jnp.exp(a) - jnp.exp(b) + lam_init


def _rope_head(xh, cos, sin_lo, sin_hi):
    up = pltpu.roll(xh, LANES - ROT_HALF, axis=1)
    down = pltpu.roll(xh, ROT_HALF, axis=1)
    return xh * cos + up * sin_lo + down * sin_hi


def _proj_kernel(x_ref, g_ref, w_ref, cos_ref, slo_ref, shi_ref, *out_refs, transposed):
    width = N_HEADS * HEAD_W
    x = x_ref[...]
    ms = jnp.mean(x * x, axis=-1, keepdims=True)
    h = (x * lax.rsqrt(ms + NORM_EPS) * g_ref[...]).astype(jnp.bfloat16)

    def seg(i):
        return jnp.dot(h, w_ref[:, i * width:(i + 1) * width], preferred_element_type=jnp.float32)

    cos, slo, shi = cos_ref[...], slo_ref[...], shi_ref[...]
    q, k, v = seg(0), seg(1), seg(2)
    if transposed:
        k_ref, v_ref, ga_ref, u_ref, gp_ref, qt_ref, kb_ref, vt_ref = out_refs
    else:
        k_ref, v_ref, ga_ref, u_ref, gp_ref, q_ref = out_refs
    ga_ref[...] = seg(3)
    u_ref[...] = seg(4)
    gp_ref[...] = seg(5)
    rows = x.shape[0]
    for hd in range(N_HEADS):
        lanes = slice(hd * HEAD_W, (hd + 1) * HEAD_W)
        kh = _rope_head(k[:, lanes], cos, slo, shi)
        qh = _rope_head(q[:, lanes], cos, slo, shi) * (
            HEAD_DIM ** -0.5 * (math.log2(math.e) if transposed else 1.0))
        k_ref[pl.ds(hd, rows, stride=N_HEADS), :] = kh
        v_ref[pl.ds(hd, rows, stride=N_HEADS), :] = v[:, lanes]
        if transposed:
            kb_ref[hd] = kh.astype(jnp.bfloat16)
            vt_ref[hd, 0] = v[:, lanes].T.astype(jnp.bfloat16)
            qt = qh.T.astype(jnp.bfloat16)
            first_half = lax.broadcasted_iota(jnp.int32, qt.shape, 0) < HEAD_DIM
            zero = jnp.zeros_like(qt)
            qt_ref[hd, 0, :, :rows] = jnp.where(first_half, qt, zero)
            qt_ref[hd, 0, :, rows:] = jnp.where(first_half, zero, qt)
        else:
            q_ref[:, lanes] = qh.astype(jnp.bfloat16)


def _rope_tables(pos):
    inv = ROPE_THETA ** (-jnp.arange(0, ROT_DIM, 2, dtype=jnp.float32) / ROT_DIM)
    ang = pos.astype(jnp.float32)[:, None] * inv[None, :]
    d = jnp.arange(LANES) % HEAD_DIM
    idx = d % ROT_HALF
    cos = jnp.where(d[None, :] < ROT_DIM, jnp.cos(ang)[:, idx], 1.0)
    sin = jnp.sin(ang)[:, idx]
    sin_lo = jnp.where(d[None, :] < ROT_HALF, -sin, 0.0)
    sin_hi = jnp.where((d[None, :] >= ROT_HALF) & (d[None, :] < ROT_DIM), sin, 0.0)
    return cos, sin_lo, sin_hi


def _project(x2d, norm_g, w_in_bf16, tables, n_table_blocks, *, tm, transposed):
    rows, d_model = x2d.shape
    width = N_HEADS * HEAD_W
    in_width = w_in_bf16.shape[1]
    f32, bf16 = jnp.float32, jnp.bfloat16
    row_spec = pl.BlockSpec((tm, width), lambda i: (i, 0))
    tab_spec = pl.BlockSpec((tm, LANES), lambda i: (i % n_table_blocks, 0))
    kv_spec = pl.BlockSpec((tm * N_HEADS, HEAD_W), lambda i: (i, 0))
    out_shape = ([jax.ShapeDtypeStruct((rows * N_HEADS, HEAD_W), f32)] * 2
                 + [jax.ShapeDtypeStruct((rows, width), f32)] * 3)
    out_specs = [kv_spec] * 2 + [row_spec] * 3
    if transposed:
        out_shape += [
            jax.ShapeDtypeStruct((N_HEADS, rows // tm, HEAD_W, 2 * tm), bf16),
            jax.ShapeDtypeStruct((N_HEADS, rows, HEAD_W), bf16),
            jax.ShapeDtypeStruct((N_HEADS, rows // tm, HEAD_W, tm), bf16),
        ]
        out_specs += [
            pl.BlockSpec((N_HEADS, 1, HEAD_W, 2 * tm), lambda i: (0, i, 0, 0)),
            pl.BlockSpec((N_HEADS, tm, HEAD_W), lambda i: (0, i, 0)),
            pl.BlockSpec((N_HEADS, 1, HEAD_W, tm), lambda i: (0, i, 0, 0)),
        ]
    else:
        out_shape += [jax.ShapeDtypeStruct((rows, width), bf16)]
        out_specs += [row_spec]
    return pl.pallas_call(
        functools.partial(_proj_kernel, transposed=transposed),
        grid=(rows // tm,),
        in_specs=[
            pl.BlockSpec((tm, d_model), lambda i: (i, 0)),
            pl.BlockSpec((1, d_model), lambda i: (0, 0)),
            pl.BlockSpec((d_model, in_width), lambda i: (0, 0)),
            tab_spec, tab_spec, tab_spec,
        ],
        out_specs=out_specs,
        out_shape=out_shape,
        compiler_params=_params("parallel"),
        name="proj_t" if transposed else "proj_n",
    )(x2d, norm_g.reshape(1, d_model), w_in_bf16, *tables)


def _attn_prompt_kernel(lq1_ref, lk1_ref, lq2_ref, lk2_ref, g_ref, qt_ref, k_ref, vt_ref, o_ref,
                        s_ref, mc_ref, m_ref, l_ref, acc_ref, bias_ref, *, t, lam_init):
    nh, nq = qt_ref.shape[0], qt_ref.shape[1]
    n_items = nq * (nq + 1) // 2
    bf16 = jnp.bfloat16

    kchunk = lax.broadcasted_iota(jnp.int32, (t, 2 * t), 0) // CHUNK
    col = lax.broadcasted_iota(jnp.int32, (t, 2 * t), 1)
    qchunk = jnp.where(col >= t, col - t, col) // CHUNK
    bias_ref[...] = jnp.where(kchunk <= qchunk, 0.0, NEG_INF)
    acc_ref[...] = jnp.zeros_like(acc_ref)
    l_ref[...] = jnp.zeros_like(l_ref)

    blocks = [(hd, slice(c * MXU_COLS, (c + 1) * MXU_COLS))
              for hd in range(nh) for c in range(2 * t // MXU_COLS)]

    nb = len(blocks)
    ORDER = [('s', 0)] + [x for b in range(nb - 1) for x in (('a', b), ('s', b + 1))] + [('a', nb - 1)]

    def scores(qa, ka, slot, hd, cols, masked):
        koff = pl.multiple_of(ka * t, t)
        st = jnp.dot(k_ref[hd, pl.ds(koff, t), :], qt_ref[hd, qa, :, cols],
                     preferred_element_type=jnp.float32)
        if masked:
            st = st + bias_ref[:, cols]
        s_ref[slot, hd, :, cols] = st
        mc_ref[slot, hd, :, cols] = jnp.max(st, axis=0, keepdims=True)

    def finish_tile(qb):
        lam = _lam(lq1_ref, lk1_ref, lq2_ref, lk2_ref, lam_init)
        rows = pl.ds(pl.multiple_of(qb * t, t), t)
        for hd in range(nh):
            acc = acc_ref[hd] / l_ref[hd]
            ot = acc[:, :t] - lam * acc[:, t:]
            ms = jnp.mean(ot * ot, axis=0, keepdims=True)
            y = ot * lax.rsqrt(ms + SUBLN_EPS) * g_ref[...] * (1.0 - lam_init)
            o_ref[rows, hd * HEAD_W:(hd + 1) * HEAD_W] = y.T

    def accumulate(qb, kb, slot, hd, cols):
        m_prev = jnp.where(kb == 0, M_FLOOR, m_ref[hd, :, cols])
        m_new = jnp.maximum(m_prev, mc_ref[slot, hd, :, cols])
        alpha = jnp.exp2(m_prev - m_new)
        pt = jnp.exp2(s_ref[slot, hd, :, cols] - m_new)
        l_ref[hd, :, cols] = alpha * l_ref[hd, :, cols] + jnp.sum(pt, axis=0, keepdims=True)
        m_ref[hd, :, cols] = m_new
        pv = jnp.dot(vt_ref[hd, kb], pt.astype(bf16), preferred_element_type=jnp.float32)
        acc_ref[hd, :, cols] = alpha * acc_ref[hd, :, cols] + pv

    def following(q, k):
        wrap = k >= q
        qn = jnp.where(wrap, q + 1, q)
        kn = jnp.where(wrap, 0, k + 1)
        done = qn >= nq
        return jnp.where(done, nq - 1, qn), jnp.where(done, nq - 1, kn)

    def iteration(item, prev, slot):
        def body(masked):
            for stage, b in ORDER:
                hd, cols = blocks[b]
                if stage == 's':
                    scores(*item, slot, hd, cols, masked)
                else:
                    accumulate(*prev, 1 - slot, hd, cols)

        lax.cond(item[0] == item[1], lambda: body(True), lambda: body(False))
        pl.when(prev[1] == prev[0])(lambda: finish_tile(prev[0]))
        return following(*item), item

    def pair(_, carry):
        item, prev = (carry[0], carry[1]), (carry[2], carry[3])
        item, prev = iteration(item, prev, 1)
        item, prev = iteration(item, prev, 0)
        return (*item, *prev)

    zero = jnp.int32(0)
    for hd, cols in blocks:
        scores(zero, zero, 0, hd, cols, True)
    carry = lax.fori_loop(0, n_items // 2, pair, (*following(zero, zero), zero, zero))
    if n_items % 2:
        iteration((carry[0], carry[1]), (carry[2], carry[3]), 1)


def _attend_prompt(lams, subln_g, qt, kb, vt, *, batch, seq, lam_init):
    t, nh = ATTN_TILE, ATTN_HEADS
    rows = batch * seq
    nq = seq // t
    lam_spec = pl.BlockSpec((1, HEAD_DIM), lambda b, h: (0, 0))
    once = pl.Buffered(1)
    return pl.pallas_call(
        functools.partial(_attn_prompt_kernel, t=t, lam_init=lam_init),
        grid=(batch, N_HEADS // nh),
        in_specs=[
            lam_spec, lam_spec, lam_spec, lam_spec,
            pl.BlockSpec((HEAD_W, 1), lambda b, h: (0, 0)),
            pl.BlockSpec((nh, nq, HEAD_W, 2 * t), lambda b, h: (h, b, 0, 0), pipeline_mode=once),
            pl.BlockSpec((nh, seq, HEAD_W), lambda b, h: (h, b, 0), pipeline_mode=once),
            pl.BlockSpec((nh, nq, HEAD_W, t), lambda b, h: (h, b, 0, 0), pipeline_mode=once),
        ],
        out_specs=pl.BlockSpec((seq, nh * HEAD_W), lambda b, h: (b, h)),
        out_shape=jax.ShapeDtypeStruct((rows, N_HEADS * HEAD_W), jnp.float32),
        scratch_shapes=[
            pltpu.VMEM((2, nh, t, 2 * t), jnp.float32),
            pltpu.VMEM((2, nh, 1, 2 * t), jnp.float32),
            pltpu.VMEM((nh, 1, 2 * t), jnp.float32),
            pltpu.VMEM((nh, 1, 2 * t), jnp.float32),
            pltpu.VMEM((nh, HEAD_W, 2 * t), jnp.float32),
            pltpu.VMEM((t, 2 * t), jnp.float32),
        ],
        compiler_params=_params("parallel", "parallel"),
        name="attn_prompt",
    )(*lams, subln_g.reshape(HEAD_W, 1), qt, kb, vt)


def _attn_sample_kernel(lq1_ref, lk1_ref, lq2_ref, lk2_ref, g_ref, q_ref, ck_ref, cv_ref, kn_ref,
                        vn_ref, o_ref, *, lam_init):
    bf16 = jnp.bfloat16
    lam = _lam(lq1_ref, lk1_ref, lq2_ref, lk2_ref, lam_init)
    ds = q_ref.shape[0]
    past = ck_ref.shape[0] // N_HEADS

    def head_rows(ref, hd, n):
        return ref[pl.ds(hd, n, stride=N_HEADS), :].astype(bf16)

    for hd in range(N_HEADS):
        lanes = slice(hd * HEAD_W, (hd + 1) * HEAD_W)
        qh = q_ref[:, lanes]
        lane = lax.broadcasted_iota(jnp.int32, qh.shape, 1)
        zero = jnp.zeros_like(qh)
        q2 = jnp.concatenate([jnp.where(lane < HEAD_DIM, qh, zero),
                              jnp.where(lane >= HEAD_DIM, qh, zero)], axis=0)
        s_c = lax.dot_general(q2, head_rows(ck_ref, hd, past), _NT,
                              preferred_element_type=jnp.float32)
        s_n = lax.dot_general(q2, head_rows(kn_ref, hd, ds), _NT,
                              preferred_element_type=jnp.float32)
        m = jnp.maximum(jnp.max(s_c, axis=-1, keepdims=True), jnp.max(s_n, axis=-1, keepdims=True))
        p_c = jnp.exp(s_c - m)
        p_n = jnp.exp(s_n - m)
        l = jnp.sum(p_c, axis=-1, keepdims=True) + jnp.sum(p_n, axis=-1, keepdims=True)
        acc = (jnp.dot(p_c.astype(bf16), head_rows(cv_ref, hd, past), preferred_element_type=jnp.float32)
               + jnp.dot(p_n.astype(bf16), head_rows(vn_ref, hd, ds), preferred_element_type=jnp.float32))
        acc = acc / l
        o = acc[:ds] - lam * acc[ds:]
        ms = jnp.mean(o * o, axis=-1, keepdims=True)
        o_ref[:, lanes] = o * lax.rsqrt(ms + SUBLN_EPS) * g_ref[...] * (1.0 - lam_init)


def _attend_sample(lams, subln_g, q, cache_k, cache_v, k_new, v_new, *, layer, lam_init):
    _, db, past, _, _ = cache_k.shape
    width = N_HEADS * HEAD_W
    ds = q.shape[0] // db
    assert (past + ds - 1) // CHUNK <= past // CHUNK
    lam_spec = pl.BlockSpec((1, HEAD_DIM), lambda b: (0, 0))
    row_spec = pl.BlockSpec((ds, width), lambda b: (b, 0))
    new_spec = pl.BlockSpec((ds * N_HEADS, HEAD_W), lambda b: (b, 0))
    cache_spec = pl.BlockSpec((None, None, past * N_HEADS, HEAD_W), lambda b: (layer, b, 0, 0))
    slab = lambda c: c.reshape(c.shape[0], db, past * N_HEADS, HEAD_W)
    return pl.pallas_call(
        functools.partial(_attn_sample_kernel, lam_init=lam_init),
        grid=(db,),
        in_specs=[lam_spec, lam_spec, lam_spec, lam_spec,
                  pl.BlockSpec((1, HEAD_W), lambda b: (0, 0)),
                  row_spec, cache_spec, cache_spec, new_spec, new_spec],
        out_specs=row_spec,
        out_shape=jax.ShapeDtypeStruct((db * ds, width), jnp.float32),
        compiler_params=_params("parallel"),
        name="attn_sample",
    )(*lams, subln_g.reshape(1, HEAD_W), q, slab(cache_k), slab(cache_v), k_new, v_new)


def _mix_out_kernel(x_ref, o_ref, ga_ref, u_ref, hist_ref, gp_ref, wpool_ref, pscale_ref, wout_ref,
                    fg_ref, y_ref, z_ref, *, zero_first_hist, pos_base, pos_stride, final):
    nb, rows, _ = u_ref.shape
    hist = hist_ref[...]
    if zero_first_hist:
        hist = jnp.where(pl.program_id(1) == 0, 0.0, hist)
    u = u_ref[...]
    z_ref[:, :HIST_ROWS, :] = hist
    z_ref[:, HIST_ROWS:, :] = u
    pos = (pos_base + pl.program_id(1) * pos_stride
           + lax.broadcasted_iota(jnp.int32, (nb, rows, LANES), 1))
    parts = []
    for g, w in enumerate(POOL_WINDOWS):
        lanes = slice(g * LANES, (g + 1) * LANES)
        ug = u[:, :, lanes]
        wsum = ug
        for back in range(1, w):
            wsum = wsum + z_ref[:, HIST_ROWS - back:HIST_ROWS - back + rows, lanes]
        cnt = jnp.minimum(pos + 1, w).astype(jnp.float32)
        mg = (wsum / cnt - ug).reshape(nb * rows, LANES).astype(jnp.bfloat16)
        parts.append(jnp.dot(mg, wpool_ref[g], preferred_element_type=jnp.float32))
    p_out = jnp.concatenate(parts, axis=-1) * pscale_ref[...] * _silu(gp_ref[...])
    a_out = o_ref[...] * _silu(ga_ref[...])
    mixed = jnp.concatenate([a_out, p_out], axis=-1).astype(jnp.bfloat16)
    y = x_ref[...] + jnp.dot(mixed, wout_ref[...], preferred_element_type=jnp.float32)
    if final:
        ms = jnp.mean(y * y, axis=-1, keepdims=True)
        y = y * lax.rsqrt(ms + NORM_EPS) * fg_ref[...]
    y_ref[...] = y


def _mix_out(x2d, o, ga, u, hist, gp, w_pool_bf16, pool_scale, w_out_bf16, final_g, *, n_seq, seq,
             nb, rows, hist_from_u, pos_base, final):
    d_model = x2d.shape[1]
    width = N_HEADS * HEAD_W
    g0, g1 = n_seq // nb, seq // rows
    tm = nb * rows
    u3 = u.reshape(n_seq, seq, width)
    row_map = lambda a, b: (a * g1 + b, 0)
    const2 = lambda a, b: (0, 0)
    if hist_from_u:
        hist, hist_blocks = u3, rows // HIST_ROWS
        hist_spec = pl.BlockSpec((nb, HIST_ROWS, width),
                                 lambda a, b: (a, jnp.maximum(b * hist_blocks - 1, 0), 0))
    else:
        hist_spec = pl.BlockSpec((nb, HIST_ROWS, width), lambda a, b: (a, 0, 0))
    return pl.pallas_call(
        functools.partial(_mix_out_kernel, zero_first_hist=hist_from_u, pos_base=pos_base,
                          pos_stride=rows if hist_from_u else 0, final=final),
        grid=(g0, g1),
        in_specs=[
            pl.BlockSpec((tm, d_model), row_map),
            pl.BlockSpec((tm, width), row_map),
            pl.BlockSpec((tm, width), row_map),
            pl.BlockSpec((nb, rows, width), lambda a, b: (a, b, 0)),
            hist_spec,
            pl.BlockSpec((tm, width), row_map),
            pl.BlockSpec(w_pool_bf16.shape, lambda a, b: (0, 0, 0)),
            pl.BlockSpec((1, width), const2),
            pl.BlockSpec(w_out_bf16.shape, const2),
            pl.BlockSpec((1, d_model), const2),
        ],
        out_specs=pl.BlockSpec((tm, d_model), row_map),
        out_shape=jax.ShapeDtypeStruct(x2d.shape, jnp.float32),
        scratch_shapes=[pltpu.VMEM((nb, HIST_ROWS + rows, width), jnp.float32)],
        compiler_params=_params("parallel", "arbitrary"),
        name="mix_out",
    )(x2d, o, ga, u3, hist, gp, w_pool_bf16, pool_scale.reshape(1, width), w_out_bf16,
      final_g.reshape(1, d_model))


def kernel(x_prompt, x_sample, cache_k, cache_v, state_pool, norm_g, w_in, lambda_q1, lambda_k1,
           lambda_q2, lambda_k2, subln_g, w_pool, pool_scale, w_out, final_g):
    batch, seq, d_model = x_prompt.shape
    db, ds, _ = x_sample.shape
    depth = w_in.shape[0]
    past = cache_k.shape[2]
    width = N_HEADS * HEAD_W
    bf16 = jnp.bfloat16

    tm_s = SAMPLE_SEQS * ds
    tab_p = _rope_tables(jnp.arange(seq, dtype=jnp.int32))
    tab_s = _rope_tables(past + (jnp.arange(tm_s, dtype=jnp.int32) % ds))

    xp = x_prompt.reshape(batch * seq, d_model)
    xs = x_sample.reshape(db * ds, d_model)
    outs = {name: [] for name in ("kp", "vp", "pp", "ks", "vs", "ps")}
    for layer in range(depth):
        lam_init = 0.8 - 0.6 * math.exp(-0.3 * layer)
        final = layer == depth - 1
        w_in_b = w_in[layer].astype(bf16)
        w_pool_b = w_pool[layer].astype(bf16)
        w_out_b = w_out[layer].astype(bf16)
        lams = [p[layer].reshape(1, HEAD_DIM) for p in (lambda_q1, lambda_k1, lambda_q2, lambda_k2)]

        k, v, ga, u, gp, qt, kb, vt = _project(
            xp, norm_g[layer], w_in_b, tab_p, seq // ATTN_TILE, tm=ATTN_TILE, transposed=True)
        o = _attend_prompt(lams, subln_g[layer], qt, kb, vt, batch=batch, seq=seq, lam_init=lam_init)
        xp = _mix_out(xp, o, ga, u, None, gp, w_pool_b, pool_scale[layer], w_out_b, final_g,
                      n_seq=batch, seq=seq, nb=1, rows=PROJ_ROWS, hist_from_u=True, pos_base=0, final=final)
        outs["kp"].append(k.reshape(batch, seq, N_HEADS, HEAD_W))
        outs["vp"].append(v.reshape(batch, seq, N_HEADS, HEAD_W))
        outs["pp"].append(u.reshape(batch, seq, width)[:, seq - POOL_HIST:])

        k, v, ga, u, gp, q = _project(
            xs, norm_g[layer], w_in_b, tab_s, 1, tm=tm_s, transposed=False)
        o = _attend_sample(lams, subln_g[layer], q, cache_k, cache_v, k, v, layer=layer, lam_init=lam_init)
        hist = jnp.pad(state_pool[layer], ((0, 0), (HIST_ROWS - POOL_HIST, 0), (0, 0)))
        xs = _mix_out(xs, o, ga, u, hist, gp, w_pool_b, pool_scale[layer], w_out_b, final_g,
                      n_seq=db, seq=ds, nb=SAMPLE_SEQS, rows=ds, hist_from_u=False, pos_base=past,
                      final=final)
        outs["ks"].append(k.reshape(db, ds, N_HEADS, HEAD_W))
        outs["vs"].append(v.reshape(db, ds, N_HEADS, HEAD_W))
        outs["ps"].append(u.reshape(db, ds, width)[:, ds - POOL_HIST:])

    stack = lambda name: jnp.stack(outs[name], axis=0)
    return (xp.reshape(batch, seq, d_model), xs.reshape(db, ds, d_model), stack("kp"), stack("vp"),
            stack("pp"), stack("ks"), stack("vs"), stack("ps"))
```
